```python
import jax, jax.numpy as jnp
from jax import lax
import numpy as np

D_MODEL = 1024
BATCH = 4
SEQ = 8192
DEPTH = 2

CTX_LEN = 256
GRID_W = 64
N_EVEN = (DEPTH + 1) // 2
N_ODD = DEPTH // 2
N_MOD = 6
EPS = 1e-6

MIX_WIDTH = D_MODEL
FOURIER_WIDTH = MIX_WIDTH // 2
FOURIER_GROUPS = 4
FOURIER_GROUP_DIM = FOURIER_WIDTH // FOURIER_GROUPS
GLA_HEADS = 4
GLA_VAL_WIDTH = MIX_WIDTH - FOURIER_WIDTH
GLA_DV = GLA_VAL_WIDTH // GLA_HEADS
GLA_DK = GLA_DV // 2
GLA_KEY_WIDTH = GLA_HEADS * GLA_DK
GLA_GATE_RANK = 16
GLA_GATE_TEMP = 16.0
GLA_CHUNK = 64
EVEN_SPLITS = [FOURIER_WIDTH,
               FOURIER_WIDTH + GLA_KEY_WIDTH,
               FOURIER_WIDTH + 2 * GLA_KEY_WIDTH,
               FOURIER_WIDTH + 2 * GLA_KEY_WIDTH + GLA_VAL_WIDTH,
               FOURIER_WIDTH + 2 * GLA_KEY_WIDTH + 2 * GLA_VAL_WIDTH]
EVEN_IN_WIDTH = EVEN_SPLITS[-1] + GLA_GATE_RANK

HEAD_DIM = 128
ATT_HEADS = D_MODEL // HEAD_DIM
ATT_KV_HEADS = 2
ATT_GROUP = ATT_HEADS // ATT_KV_HEADS
Q_WIDTH = ATT_HEADS * HEAD_DIM
QKV_WIDTH = (ATT_HEADS + 2 * ATT_KV_HEADS) * HEAD_DIM
Q_BLOCK = 128
ROPE_THETA = 10000.0
ROPE_AXIS_DIM = HEAD_DIM // 2

D_FF = ((8 * D_MODEL // 3 + 255) // 256) * 256
CONV_W = 3

kernel_name = 'hybrid_fourier_gla_gqa_dit_prefix'


def rms_norm(x, g):
    xf = x.astype(jnp.float32)
    y = xf * lax.rsqrt(jnp.mean(xf * xf, axis=-1, keepdims=True) + EPS)
    return (y * g.astype(jnp.float32)).astype(x.dtype)


def adaln(cv, w, b):
    m = jax.nn.silu(cv) @ w + b
    m = m.reshape(cv.shape[:-1] + (1, N_MOD, D_MODEL))
    return [m[..., k, :] for k in range(N_MOD)]


def dwconv_centred(x, w, b):
    L = x.shape[1]
    pad = CONV_W // 2
    xp = jnp.pad(x, ((0, 0), (pad, CONV_W - 1 - pad), (0, 0)))
    out = b
    for k in range(CONV_W):
        out = out + xp[:, k:k + L] * w[k]
    return out


def conv_ffn(h, w_up, w_conv, b_conv, w_down):
    u = dwconv_centred(h @ w_up, w_conv, b_conv)
    g, v = jnp.split(u, 2, axis=-1)
    return (jax.nn.silu(g) * v) @ w_down


def fourier_mix(u):
    B_, L, _ = u.shape
    ug = u.reshape(B_, L, FOURIER_GROUPS, FOURIER_GROUP_DIM).astype(jnp.float32)
    f = jnp.fft.fft2(ug, axes=(1, 3), norm='ortho').real
    return f.reshape(B_, L, FOURIER_WIDTH).astype(u.dtype)


def gla_scan(q, k, v, loga, s0):
    q, k, v = (t.astype(jnp.float32) for t in (q, k, v))
    B_, H, L, _ = q.shape
    n = L // GLA_CHUNK

    def to_chunks(t):
        return jnp.moveaxis(t.reshape(B_, H, n, GLA_CHUNK, t.shape[-1]), 2, 0)

    mask = jnp.tril(jnp.ones((GLA_CHUNK, GLA_CHUNK), bool))[:, :, None]

    def step(S, inp):
        qc, kc, vc, ac = inp
        b = jnp.cumsum(ac, axis=2)
        diff = b[:, :, :, None, :] - b[:, :, None, :, :]
        decay = jnp.exp(jnp.where(mask, diff, -jnp.inf))
        attn = jnp.einsum('bhid,bhjd,bhijd->bhij', qc, kc, decay)
        o = attn @ vc + jnp.einsum('bhid,bhde->bhie', qc * jnp.exp(b), S)
        b_last = b[:, :, -1:, :]
        S_new = (jnp.exp(b_last[:, :, 0, :, None]) * S
                 + jnp.einsum('bhjd,bhje->bhde', kc * jnp.exp(b_last - b), vc))
        return S_new, o

    s_fin, o = lax.scan(step, s0, (to_chunks(q), to_chunks(k), to_chunks(v), to_chunks(loga)))
    o = jnp.moveaxis(o, 0, 2).reshape(B_, H, L, v.shape[-1])
    return o, s_fin


def even_mix(h, s_f, s_b, w_in, w_gate, b_gate, g_gla, w_out):
    B_, L, _ = h.shape
    p = h @ w_in
    u_f, q, k, v, r, z = jnp.split(p, EVEN_SPLITS, axis=-1)

    def heads(t, d):
        return t.reshape(B_, L, GLA_HEADS, d).transpose(0, 2, 1, 3)

    q = heads(q, GLA_DK) * (GLA_DK ** -0.5)
    k = heads(k, GLA_DK)
    v = heads(v, GLA_DV)
    zf = z.astype(jnp.float32)
    loga = [heads(jax.nn.log_sigmoid(zf @ w_gate[d].astype(jnp.float32) + b_gate[d].astype(jnp.float32))
                  / GLA_GATE_TEMP, GLA_DK) for d in range(2)]

    def flip(t):
        return jnp.flip(t, axis=2)

    o_f, s_f = gla_scan(q, k, v, loga[0], s_f)
    o_b, s_b = gla_scan(flip(q), flip(k), flip(v), flip(loga[1]), s_b)
    o = o_f + flip(o_b)
    o = o * lax.rsqrt(jnp.mean(o * o, axis=-1, keepdims=True) + EPS)
    o = o.transpose(0, 2, 1, 3).reshape(B_, L, GLA_VAL_WIDTH) * g_gla.astype(jnp.float32)
    o = (o * jax.nn.silu(r.astype(jnp.float32))).astype(h.dtype)
    y = jnp.concatenate([fourier_mix(u_f), o], axis=-1) @ w_out
    return y, s_f, s_b


def axial_rope_tables(n_tokens):
    rows = n_tokens // GRID_W
    r, c = jnp.meshgrid(jnp.arange(rows), jnp.arange(GRID_W), indexing='ij')
    inv = ROPE_THETA ** (-jnp.arange(0, ROPE_AXIS_DIM, 2, dtype=jnp.float32) / ROPE_AXIS_DIM)
    ang = jnp.concatenate([r.reshape(-1, 1).astype(jnp.float32) * inv,
                           c.reshape(-1, 1).astype(jnp.float32) * inv], axis=-1)
    return jnp.cos(ang), jnp.sin(ang)


def apply_rope(t, cos, sin):
    tf = t.astype(jnp.float32)
    t1, t2 = jnp.split(tf, 2, axis=-1)
    return jnp.concatenate([t1 * cos - t2 * sin, t1 * sin + t2 * cos], axis=-1).astype(t.dtype)


def attn_q(h, w_q, g_q):
    B_, L, _ = h.shape
    q = rms_norm((h @ w_q).reshape(B_, L, ATT_HEADS, HEAD_DIM), g_q)
    return q.transpose(0, 2, 1, 3)


def attn_kv(h, w_kv, g_k):
    B_, L, _ = h.shape
    kv = (h @ w_kv).reshape(B_, L, 2, ATT_KV_HEADS, HEAD_DIM)
    k = rms_norm(kv[:, :, 0], g_k).transpose(0, 2, 1, 3)
    v = kv[:, :, 1].transpose(0, 2, 1, 3)
    return k, v


def grouped_attend(q, k, v):
    s = jnp.einsum('bkgqd,bksd->bkgqs', q, k).astype(jnp.float32) * (HEAD_DIM ** -0.5)
    p = jax.nn.softmax(s, axis=-1).astype(v.dtype)
    return jnp.einsum('bkgqs,bksd->bkgqd', p, v)


def setup_inputs(seed: int = 0) -> dict:
    key = jax.random.key(seed)
    ks = jax.random.split(key, 22)
    f32 = jnp.float32

    def nrm(k, shape, scale):
        return jax.random.normal(k, shape, f32) * scale

    def gain(k, shape):
        return 1.0 + 0.02 * jax.random.normal(k, shape, f32)

    return {
        'x': nrm(ks[0], (BATCH, SEQ, D_MODEL), 1.0),
        'c': nrm(ks[1], (BATCH, D_MODEL), 1.0),
        'ctx': nrm(ks[2], (BATCH, CTX_LEN, D_MODEL), 1.0),
        'c_ctx': nrm(ks[3], (D_MODEL,), 1.0),
        'w_mod': nrm(ks[4], (DEPTH, D_MODEL, N_MOD * D_MODEL), D_MODEL ** -0.5),
        'b_mod': nrm(ks[5], (DEPTH, N_MOD * D_MODEL), 0.02),
        'g_norm_mix': gain(ks[6], (DEPTH, D_MODEL)),
        'g_norm_ffn': gain(ks[7], (DEPTH, D_MODEL)),
        'g_norm_final': gain(ks[8], (D_MODEL,)),
        'w_even_in': nrm(ks[9], (N_EVEN, D_MODEL, EVEN_IN_WIDTH), D_MODEL ** -0.5),
        'w_gla_gate': nrm(ks[10], (N_EVEN, 2, GLA_GATE_RANK, GLA_KEY_WIDTH), GLA_GATE_RANK ** -0.5),
        'b_gla_gate': nrm(ks[11], (N_EVEN, 2, GLA_KEY_WIDTH), 0.1),
        'g_gla_out': gain(ks[12], (N_EVEN, GLA_VAL_WIDTH)),
        'w_even_out': nrm(ks[13], (N_EVEN, MIX_WIDTH, D_MODEL), MIX_WIDTH ** -0.5),
        'w_qkv': nrm(ks[14], (N_ODD, D_MODEL, QKV_WIDTH), D_MODEL ** -0.5),
        'g_q': gain(ks[15], (N_ODD, HEAD_DIM)),
        'g_k': gain(ks[16], (N_ODD, HEAD_DIM)),
        'w_att_out': nrm(ks[17], (N_ODD, Q_WIDTH, D_MODEL), Q_WIDTH ** -0.5),
        'w_ffn_up': nrm(ks[18], (DEPTH, D_MODEL, 2 * D_FF), D_MODEL ** -0.5),
        'w_ffn_conv': nrm(ks[19], (DEPTH, CONV_W, 2 * D_FF), CONV_W ** -0.5),
        'b_ffn_conv': nrm(ks[20], (DEPTH, 2 * D_FF), 0.02),
        'w_ffn_down': nrm(ks[21], (DEPTH, D_FF, D_MODEL), D_FF ** -0.5),
    }


def reference(x, c, ctx, c_ctx, w_mod, b_mod, g_norm_mix, g_norm_ffn, g_norm_final,
              w_even_in, w_gla_gate, b_gla_gate, g_gla_out, w_even_out,
              w_qkv, g_q, g_k, w_att_out,
              w_ffn_up, w_ffn_conv, b_ffn_conv, w_ffn_down):
    B_, S, _ = x.shape
    cos, sin = axial_rope_tables(S)
    h_lat, h_ctx = x, ctx
    for i in range(DEPTH):
        last = i == DEPTH - 1
        j = i // 2
        sh1, sc1, ga1, sh2, sc2, ga2 = adaln(c, w_mod[i], b_mod[i])
        csh1, csc1, cga1, csh2, csc2, cga2 = adaln(c_ctx, w_mod[i], b_mod[i])
        a_lat = rms_norm(h_lat, g_norm_mix[i]) * (1 + sc1) + sh1
        a_ctx = rms_norm(h_ctx, g_norm_mix[i]) * (1 + csc1) + csh1
        if i % 2 == 0:
            zero = jnp.zeros((B_, GLA_HEADS, GLA_DK, GLA_DV), jnp.float32)
            o_ctx, s_f, s_b = even_mix(a_ctx, zero, zero, w_even_in[j], w_gla_gate[j],
                                       b_gla_gate[j], g_gla_out[j], w_even_out[j])
            o_lat, _, _ = even_mix(a_lat, s_f, s_b, w_even_in[j], w_gla_gate[j],
                                   b_gla_gate[j], g_gla_out[j], w_even_out[j])
        else:
            w_q = w_qkv[j][:, :Q_WIDTH]
            w_kv = w_qkv[j][:, Q_WIDTH:]
            q_l = apply_rope(attn_q(a_lat, w_q, g_q[j]), cos, sin)
            k_l, v_l = attn_kv(a_lat, w_kv, g_k[j])
            k_l = apply_rope(k_l, cos, sin)
            k_c, v_c = attn_kv(a_ctx, w_kv, g_k[j])
            k_all = jnp.concatenate([k_c, k_l], axis=2)
            v_all = jnp.concatenate([v_c, v_l], axis=2)
            nb = S // Q_BLOCK
            qb = jnp.moveaxis(q_l.reshape(B_, ATT_KV_HEADS, ATT_GROUP, nb, Q_BLOCK, HEAD_DIM), 3, 0)
            ob = lax.map(lambda blk: grouped_attend(blk, k_all, v_all), qb)
            o_lat = ob.transpose(1, 0, 4, 2, 3, 5).reshape(B_, S, Q_WIDTH) @ w_att_out[j]
            if not last:
                Lc = a_ctx.shape[1]
                q_c = attn_q(a_ctx, w_q, g_q[j]).reshape(B_, ATT_KV_HEADS, ATT_GROUP, Lc, HEAD_DIM)
                o_c = grouped_attend(q_c, k_c, v_c)
                o_ctx = o_c.transpose(0, 3, 1, 2, 4).reshape(B_, Lc, Q_WIDTH) @ w_att_out[j]
        h_lat = h_lat + ga1 * o_lat
        f_lat = rms_norm(h_lat, g_norm_ffn[i]) * (1 + sc2) + sh2
        h_lat = h_lat + ga2 * conv_ffn(f_lat, w_ffn_up[i], w_ffn_conv[i], b_ffn_conv[i], w_ffn_down[i])
        if not last:
            h_ctx = h_ctx + cga1 * o_ctx
            f_ctx = rms_norm(h_ctx, g_norm_ffn[i]) * (1 + csc2) + csh2
            h_ctx = h_ctx + cga2 * conv_ffn(f_ctx, w_ffn_up[i], w_ffn_conv[i], b_ffn_conv[i], w_ffn_down[i])
    return rms_norm(h_lat, g_norm_final)
```

```python
import functools
import math

import jax
import jax.numpy as jnp
import numpy as np
from jax import lax
from jax.experimental import pallas as pl
from jax.experimental.pallas import tpu as pltpu

F32 = jnp.float32
BF16 = jnp.bfloat16

EPS = 1e-6
N_MOD = 6
LANE = 128
SUBLANE_BF16 = 16
VMEM_LIMIT = 56 * 1024 * 1024

FOURIER_GROUPS = 4
GLA_HEADS = 4
GLA_DK = 64
GLA_DV = 128
GLA_GATE_RANK = 16
GLA_GATE_TEMP = 16.0
GLA_CHUNK = 64
HEAD_DIM = 128
ATT_KV_HEADS = 2
GRID_W = 64
ROPE_THETA = 10000.0
CONV_W = 3
Q_SCALE = HEAD_DIM ** -0.5 * math.log2(math.e)

_NT = (((1,), (1,)), ((), ()))
_TN = (((0,), (0,)), ((), ()))


def _dot(a, b):
    return jnp.dot(a, b, preferred_element_type=F32)


def _params(*sem):
    return pltpu.CompilerParams(dimension_semantics=sem, vmem_limit_bytes=VMEM_LIMIT)


def _const_spec(shape):
    nd = len(shape)
    return pl.BlockSpec(shape, lambda *_: (0,) * nd, pipeline_mode=pl.Buffered(1))


def _table(values):
    return jnp.asarray(values, F32).astype(BF16)


def _row_tile(n, cap):
    t = min(n, cap)
    assert n % t == 0, (n, t)
    return t


def _norm_mod(x, g, sc, sh):
    y = x * lax.rsqrt(jnp.mean(x * x, axis=-1, keepdims=True) + EPS)
    return y * (g * (1.0 + sc)) + sh


def _silu(x):
    return x * jax.nn.sigmoid(x)


def _mod_kernel(cv_ref, w_ref, b_ref, o_ref):
    s = _silu(cv_ref[...]).astype(BF16)
    o_ref[...] = _dot(s, w_ref[...].astype(BF16)) + b_ref[...]


def _modulation(cv, w_mod, b_mod):
    depth, d, n = w_mod.shape
    rows = cv.shape[0]
    tn = 1536
    assert n % tn == 0
    return pl.pallas_call(
        _mod_kernel,
        grid=(depth, n // tn),
        in_specs=[
            pl.BlockSpec((rows, d), lambda i, j: (0, 0)),
            pl.BlockSpec((None, d, tn), lambda i, j: (i, 0, j)),
            pl.BlockSpec((None, 1, tn), lambda i, j: (i, 0, j)),
        ],
        out_specs=pl.BlockSpec((None, rows, tn), lambda i, j: (i, 0, j)),
        out_shape=jax.ShapeDtypeStruct((depth, rows, n), F32),
        compiler_params=_params("parallel", "parallel"),
        name="adaln_mod",
    )(cv, w_mod, b_mod.reshape(depth, 1, n))


def _split3(x):
    hi = x.astype(BF16)
    r1 = x - hi.astype(F32)
    mid = r1.astype(BF16)
    lo = (r1 - mid.astype(F32)).astype(BF16)
    return hi, mid, lo


def _inproj_kernel(h_ref, sc_ref, sh_ref, g_ref, wm_ref, wz_ref, cs_ref, wg_ref, bg_ref,
                   tril_ref, triu_ref,
                   af_ref, bf_ref, q_ref, k_ref, v_ref, r_ref, cf_ref, cb_ref):
    fw = FOURIER_GROUPS * LANE
    kw = GLA_HEADS * GLA_DK
    vw = GLA_HEADS * GLA_DV
    a = _norm_mod(h_ref[...], g_ref[...], sc_ref[...], sh_ref[...]).astype(BF16)
    p = _dot(a, wm_ref[...])
    for g in range(FOURIER_GROUPS):
        ab = _dot(p[:, g * LANE:(g + 1) * LANE].astype(BF16), cs_ref[...])
        af_ref[:, g * LANE:(g + 1) * LANE] = ab[:, :LANE].astype(BF16)
        bf_ref[:, g * LANE:(g + 1) * LANE] = ab[:, LANE:].astype(BF16)
    q_ref[...] = p[:, fw:fw + kw] * (GLA_DK ** -0.5)
    k_ref[...] = p[:, fw + kw:fw + 2 * kw]
    v_ref[...] = p[:, fw + 2 * kw:fw + 2 * kw + vw].astype(BF16)
    r_ref[...] = p[:, fw + 2 * kw + vw:fw + 2 * kw + 2 * vw]
    z = _dot(a, wz_ref[...])
    zz = _dot(z.astype(BF16), wg_ref[...]) + bg_ref[...]
    loga = (jnp.minimum(zz, 0.0) - jnp.log1p(jnp.exp(-jnp.abs(zz)))) * (1.0 / GLA_GATE_TEMP)
    tril = tril_ref[...]
    triu = triu_ref[...]
    tm = h_ref.shape[0]
    for c in range(tm // GLA_CHUNK):
        rows = slice(c * GLA_CHUNK, (c + 1) * GLA_CHUNK)
        la = loga[rows]
        accf = None
        accb = None
        for term in _split3(la):
            tf = _dot(tril, term[:, :kw])
            tb = _dot(triu, term[:, kw:])
            accf = tf if accf is None else accf + tf
            accb = tb if accb is None else accb + tb
        cf_ref[rows, :] = accf
        cb_ref[rows, :] = accb


def _inproj(h, sc, sh, g, wm, wz, cs, wg, bg, tril, triu):
    b, l, d = h.shape
    tm = _row_tile(l, 512)
    per_batch = sc.shape[0] == b and b > 1
    mod_idx = (lambda bi, i: (bi, 0, 0)) if per_batch else (lambda bi, i: (0, 0, 0))
    row = lambda w: pl.BlockSpec((None, tm, w), lambda bi, i: (bi, i, 0))
    outs = [(512, BF16), (512, BF16), (256, F32), (256, F32), (512, BF16), (512, F32), (256, F32), (256, F32)]
    return pl.pallas_call(
        _inproj_kernel,
        grid=(b, l // tm),
        in_specs=[
            row(d),
            pl.BlockSpec((None, 1, d), mod_idx),
            pl.BlockSpec((None, 1, d), mod_idx),
            _const_spec(g.shape), _const_spec(wm.shape), _const_spec(wz.shape), _const_spec(cs.shape),
            _const_spec(wg.shape), _const_spec(bg.shape), _const_spec(tril.shape), _const_spec(triu.shape),
        ],
        out_specs=[row(w) for w, _ in outs],
        out_shape=[jax.ShapeDtypeStruct((b, l, w), dt) for w, dt in outs],
        compiler_params=_params("parallel", "parallel"),
        name="even_inproj",
    )(h, sc, sh, g, wm, wz, cs, wg, bg, tril, triu)


def _dft_dense_kernel(a_ref, b_ref, c_ref, s_ref, o_ref):
    o_ref[...] = (_dot(c_ref[...], a_ref[...]) - _dot(s_ref[...], b_ref[...])).astype(o_ref.dtype)


def _dft_dense(af, bf):
    b, l, w = af.shape
    n = np.arange(l)
    ang = 2.0 * np.pi * ((n[:, None] * n[None, :]) % l) / l
    c = _table(np.cos(ang) / math.sqrt(l))
    s = _table(np.sin(ang) / math.sqrt(l))
    blk = pl.BlockSpec((None, l, w), lambda bi: (bi, 0, 0))
    return pl.pallas_call(
        _dft_dense_kernel,
        grid=(b,),
        in_specs=[blk, blk, _const_spec(c.shape), _const_spec(s.shape)],
        out_specs=blk,
        out_shape=jax.ShapeDtypeStruct((b, l, w), BF16),
        compiler_params=_params("parallel"),
        name="dft_dense",
    )(af, bf, c, s)


DFT_N2 = LANE
DFT_NB = 8
DFT_KB = 8


def _dft_stage1_kernel(a_ref, b_ref, ma_ref, mb_ref, tc_ref, ts_ref, y_ref):
    n1 = a_ref.shape[0]
    y = _dot(ma_ref[...], a_ref[...]) + _dot(mb_ref[...], b_ref[...])
    for j in range(DFT_NB):
        tc = tc_ref[:, j * LANE:(j + 1) * LANE]
        ts = ts_ref[:, j * LANE:(j + 1) * LANE]
        for g in range(FOURIER_GROUPS):
            sl = slice((j * FOURIER_GROUPS + g) * LANE, (j * FOURIER_GROUPS + g + 1) * LANE)
            yr = y[:n1, sl]
            yi = y[n1:, sl]
            y_ref[:n1, sl] = (yr * tc + yi * ts).astype(BF16)
            y_ref[n1:, sl] = (yi * tc - yr * ts).astype(BF16)


def _dft_stage2_kernel(yr_ref, yi_ref, c_ref, s_ref, o_ref):
    w = yr_ref.shape[-1]
    for kk in range(DFT_KB):
        o = _dot(c_ref[...], yr_ref[kk]) + _dot(s_ref[...], yi_ref[kk])
        o_ref[:, kk * w:(kk + 1) * w] = o.astype(o_ref.dtype)


def _dft_factored(af, bf):
    b, l, w = af.shape
    n1 = l // DFT_N2
    assert l % DFT_N2 == 0 and n1 % SUBLANE_BF16 == 0 and n1 % DFT_KB == 0
    k = np.arange(n1)
    ang1 = 2.0 * np.pi * ((k[:, None] * k[None, :]) % n1) / n1
    c1, s1 = np.cos(ang1), np.sin(ang1)
    ma = _table(np.concatenate([c1, -s1], axis=0))
    mb = _table(np.concatenate([-s1, -c1], axis=0))
    n2 = np.arange(DFT_N2)
    angt = 2.0 * np.pi * (k[:, None] * n2[None, :]) / l
    scale = 1.0 / math.sqrt(l)
    tc = jnp.asarray(np.repeat(np.cos(angt) * scale, LANE, axis=1), F32)
    ts = jnp.asarray(np.repeat(np.sin(angt) * scale, LANE, axis=1), F32)
    ang2 = 2.0 * np.pi * ((n2[:, None] * n2[None, :]) % DFT_N2) / DFT_N2
    c2 = _table(np.cos(ang2))
    s2 = _table(np.sin(ang2))

    a3 = af.reshape(b, n1, DFT_N2 * w)
    b3 = bf.reshape(b, n1, DFT_N2 * w)
    in_blk = pl.BlockSpec((None, n1, DFT_NB * w), lambda bi, j: (bi, 0, j))
    y = pl.pallas_call(
        _dft_stage1_kernel,
        grid=(b, DFT_N2 // DFT_NB),
        in_specs=[
            in_blk, in_blk, _const_spec(ma.shape), _const_spec(mb.shape),
            pl.BlockSpec((n1, DFT_NB * LANE), lambda bi, j: (0, j)),
            pl.BlockSpec((n1, DFT_NB * LANE), lambda bi, j: (0, j)),
        ],
        out_specs=pl.BlockSpec((None, 2 * n1, DFT_NB * w), lambda bi, j: (bi, 0, j)),
        out_shape=jax.ShapeDtypeStruct((b, 2 * n1, DFT_N2 * w), BF16),
        compiler_params=_params("parallel", "parallel"),
        name="dft_stage1",
    )(a3, b3, ma, mb, tc, ts)

    y4 = y.reshape(b, 2 * n1, DFT_N2, w)
    nkb = n1 // DFT_KB
    out = pl.pallas_call(
        _dft_stage2_kernel,
        grid=(b, nkb),
        in_specs=[
            pl.BlockSpec((None, DFT_KB, DFT_N2, w), lambda bi, i: (bi, i, 0, 0)),
            pl.BlockSpec((None, DFT_KB, DFT_N2, w), lambda bi, i: (bi, nkb + i, 0, 0)),
            _const_spec(c2.shape), _const_spec(s2.shape),
        ],
        out_specs=pl.BlockSpec((None, DFT_N2, DFT_KB * w), lambda bi, i: (bi, 0, i)),
        out_shape=jax.ShapeDtypeStruct((b, DFT_N2, n1 * w), BF16),
        compiler_params=_params("parallel", "parallel"),
        name="dft_stage2",
    )(y4, y4, c2, s2)
    return out.reshape(b, l, w)


def _position_dft(af, bf):
    l = af.shape[1]
    if l % (DFT_N2 * SUBLANE_BF16) == 0:
        return _dft_factored(af, bf)
    return _dft_dense(af, bf)


def _gla_chunk(q, k, v, cum, s_ref, head_masks, tri_mask, state_mask, forward):
    c = q.shape[0]
    tot = cum[c - 1:c] if forward else cum[0:1]
    qe = q * jnp.exp(cum)
    ke = (k * jnp.exp(-cum)).astype(BF16)
    kd = (k * jnp.exp(tot - cum)).astype(BF16)
    q_stack = jnp.concatenate([qe * hm for hm in head_masks], axis=0).astype(BF16)
    att = lax.dot_general(q_stack, ke, _NT, preferred_element_type=F32)
    att = jnp.where(tri_mask, att, 0.0).astype(BF16)
    o_full = _dot(att, v)
    o_intra = jnp.concatenate(
        [o_full[h * c:(h + 1) * c, h * GLA_DV:(h + 1) * GLA_DV] for h in range(GLA_HEADS)], axis=1)
    s = s_ref[...]
    o_inter = lax.dot_general(qe.astype(BF16), s.astype(BF16), _NT, preferred_element_type=F32)
    upd = lax.dot_general(v, kd, _TN, preferred_element_type=F32)
    s_ref[...] = s * jnp.exp(tot) + jnp.where(state_mask, upd, 0.0)
    return o_intra + o_inter


def _gla_kernel(qf_ref, kf_ref, vf_ref, cf_ref, qb_ref, kb_ref, vb_ref, cb_ref, s0f_ref, s0b_ref,
                of_ref, ob_ref, sf_ref, sb_ref):
    t = pl.program_id(1)
    c = GLA_CHUNK
    kw = GLA_HEADS * GLA_DK
    vw = GLA_HEADS * GLA_DV

    @pl.when(t == 0)
    def _():
        sf_ref[...] = s0f_ref[...]
        sb_ref[...] = s0b_ref[...]

    lane = lax.broadcasted_iota(jnp.int32, (1, kw), 1)
    head_masks = [(lane // GLA_DK == h).astype(F32) for h in range(GLA_HEADS)]
    ri = lax.broadcasted_iota(jnp.int32, (GLA_HEADS * c, c), 0) % c
    ci = lax.broadcasted_iota(jnp.int32, (GLA_HEADS * c, c), 1)
    tril = ci <= ri
    triu = ci >= ri
    sr = lax.broadcasted_iota(jnp.int32, (vw, kw), 0) // GLA_DV
    scol = lax.broadcasted_iota(jnp.int32, (vw, kw), 1) // GLA_DK
    state_mask = sr == scol

    nc = qf_ref.shape[0] // c
    for i in range(nc):
        rf = slice(i * c, (i + 1) * c)
        of_ref[rf, :] = _gla_chunk(qf_ref[rf, :], kf_ref[rf, :], vf_ref[rf, :], cf_ref[rf, :],
                                   sf_ref, head_masks, tril, state_mask, True)
        rb = slice((nc - 1 - i) * c, (nc - i) * c)
        ob_ref[rb, :] = _gla_chunk(qb_ref[rb, :], kb_ref[rb, :], vb_ref[rb, :], cb_ref[rb, :],
                                   sb_ref, head_masks, triu, state_mask, False)


def _gla(q, k, v, cf, cb, s0f, s0b):
    b, l, kw = q.shape
    vw = v.shape[-1]
    tl = _row_tile(l, 512)
    nt = l // tl
    fwd = lambda w: pl.BlockSpec((None, tl, w), lambda bi, t: (bi, t, 0))
    bwd = lambda w: pl.BlockSpec((None, tl, w), lambda bi, t: (bi, nt - 1 - t, 0))
    st = pl.BlockSpec((None, vw, kw), lambda bi, t: (bi, 0, 0))
    return pl.pallas_call(
        _gla_kernel,
        grid=(b, nt),
        in_specs=[fwd(kw), fwd(kw), fwd(vw), fwd(kw), bwd(kw), bwd(kw), bwd(vw), bwd(kw), st, st],
        out_specs=[fwd(vw), bwd(vw), st, st],
        out_shape=[jax.ShapeDtypeStruct((b, l, vw), F32), jax.ShapeDtypeStruct((b, l, vw), F32),
                   jax.ShapeDtypeStruct((b, vw, kw), F32), jax.ShapeDtypeStruct((b, vw, kw), F32)],
        compiler_params=_params("parallel", "arbitrary"),
        name="gla_scan",
    )(q, k, v, cf, q, k, v, cb, s0f, s0b)


def _mixout_kernel(h_ref, f_ref, of_ref, ob_ref, r_ref, gg_ref, w_ref, ga_ref, o_ref):
    fw = f_ref.shape[-1]
    o = of_ref[...] + ob_ref[...]
    parts = []
    for hd in range(GLA_HEADS):
        oh = o[:, hd * GLA_DV:(hd + 1) * GLA_DV]
        parts.append(oh * lax.rsqrt(jnp.mean(oh * oh, axis=-1, keepdims=True) + EPS))
    on = jnp.concatenate(parts, axis=1) * gg_ref[...]
    on = (on * _silu(r_ref[...])).astype(BF16)
    y = _dot(f_ref[...], w_ref[:fw, :]) + _dot(on, w_ref[fw:, :])
    o_ref[...] = h_ref[...] + ga_ref[...] * y


def _mixout(h, four, of, ob, r, gg, w, ga):
    b, l, d = h.shape
    tm = _row_tile(l, 512)
    per_batch = ga.shape[0] == b and b > 1
    mod_idx = (lambda bi, i: (bi, 0, 0)) if per_batch else (lambda bi, i: (0, 0, 0))
    row = lambda w_: pl.BlockSpec((None, tm, w_), lambda bi, i: (bi, i, 0))
    return pl.pallas_call(
        _mixout_kernel,
        grid=(b, l // tm),
        in_specs=[row(d), row(four.shape[-1]), row(of.shape[-1]), row(ob.shape[-1]), row(r.shape[-1]),
                  _const_spec(gg.shape), _const_spec(w.shape), pl.BlockSpec((None, 1, d), mod_idx)],
        out_specs=row(d),
        out_shape=jax.ShapeDtypeStruct((b, l, d), F32),
        compiler_params=_params("parallel", "parallel"),
        name="even_mixout",
    )(h, four, of, ob, r, gg, w, ga)


FFN_HALO = SUBLANE_BF16
FFN_TF = 256


def _ffn_kernel(h_ref, hp_ref, hn_ref, sc_ref, sh_ref, ga_ref, g_ref, wup_ref, wc_ref, bc_ref, wdn_ref,
                *rest, final):
    if final:
        gfin_ref, o_ref, fext_ref, p_ref = rest
    else:
        o_ref, fext_ref, p_ref = rest
    i = pl.program_id(1)
    last = pl.num_programs(1) - 1
    tm = h_ref.shape[0]
    dff = wdn_ref.shape[0]
    hal = FFN_HALO
    g, sc, sh = g_ref[...], sc_ref[...], sh_ref[...]
    h = h_ref[...]
    fp = _norm_mod(hp_ref[...], g, sc, sh) * (i > 0).astype(F32)
    fn = _norm_mod(hn_ref[...], g, sc, sh) * (i < last).astype(F32)
    fext_ref[0:hal, :] = fp.astype(BF16)
    fext_ref[hal:hal + tm, :] = _norm_mod(h, g, sc, sh).astype(BF16)
    fext_ref[hal + tm:, :] = fn.astype(BF16)
    fext = fext_ref[...]
    acc = None
    for c in range(dff // FFN_TF):
        cols = []
        for half in range(2):
            cs = slice(half * dff + c * FFN_TF, half * dff + (c + 1) * FFN_TF)
            p_ref[...] = _dot(fext, wup_ref[:, cs])
            u = (p_ref[pl.ds(hal - 1, tm), :] * wc_ref[0:1, cs]
                 + p_ref[pl.ds(hal, tm), :] * wc_ref[1:2, cs]
                 + p_ref[pl.ds(hal + 1, tm), :] * wc_ref[2:3, cs]
                 + bc_ref[:, cs])
            cols.append(u)
        act = (_silu(cols[0]) * cols[1]).astype(BF16)
        d = _dot(act, wdn_ref[c * FFN_TF:(c + 1) * FFN_TF, :])
        acc = d if acc is None else acc + d
    out = h + ga_ref[...] * acc
    if final:
        out = out * lax.rsqrt(jnp.mean(out * out, axis=-1, keepdims=True) + EPS) * gfin_ref[...]
    o_ref[...] = out


def _conv_ffn(h, sc, sh, ga, g, wup, wc, bc, wdn, gfin=None):
    b, l, d = h.shape
    tm = _row_tile(l, 512)
    assert tm % FFN_HALO == 0 and wdn.shape[0] % FFN_TF == 0 and CONV_W == 3
    per_batch = ga.shape[0] == b and b > 1
    mod_idx = (lambda bi, i: (bi, 0, 0)) if per_batch else (lambda bi, i: (0, 0, 0))
    hb = tm // FFN_HALO
    nh = l // FFN_HALO
    mod = pl.BlockSpec((None, 1, d), mod_idx)
    in_specs = [
        pl.BlockSpec((None, tm, d), lambda bi, i: (bi, i, 0)),
        pl.BlockSpec((None, FFN_HALO, d), lambda bi, i: (bi, jnp.maximum(i * hb - 1, 0), 0)),
        pl.BlockSpec((None, FFN_HALO, d), lambda bi, i: (bi, jnp.minimum((i + 1) * hb, nh - 1), 0)),
        mod, mod, mod,
        _const_spec(g.shape), _const_spec(wup.shape), _const_spec(wc.shape), _const_spec(bc.shape),
        _const_spec(wdn.shape),
    ]
    args = [h, h, h, sc, sh, ga, g, wup, wc, bc, wdn]
    if gfin is not None:
        in_specs.append(_const_spec(gfin.shape))
        args.append(gfin)
    return pl.pallas_call(
        functools.partial(_ffn_kernel, final=gfin is not None),
        grid=(b, l // tm),
        in_specs=in_specs,
        out_specs=pl.BlockSpec((None, tm, d), lambda bi, i: (bi, i, 0)),
        out_shape=jax.ShapeDtypeStruct((b, l, d), F32),
        scratch_shapes=[pltpu.VMEM((tm + 2 * FFN_HALO, d), BF16),
                        pltpu.VMEM((tm + 2 * FFN_HALO, FFN_TF), F32)],
        compiler_params=_params("parallel", "parallel"),
        name="conv_ffn_final" if gfin is not None else "conv_ffn",
    )(*args)


def _head_norm(t, g):
    return t * lax.rsqrt(jnp.mean(t * t, axis=-1, keepdims=True) + EPS) * g


def _rope(t, cosf, sinf):
    return t * cosf + pltpu.roll(t, HEAD_DIM // 2, 1) * sinf


def _qkv_kernel(h_ref, sc_ref, sh_ref, g_ref, w_ref, gq_ref, gk_ref, *rest, n_q, rope):
    if rope:
        cos_ref, sin_ref = rest[:2]
        rest = rest[2:]
    if n_q:
        q_ref, k_ref, v_ref = rest
    else:
        k_ref, v_ref = rest
    a = _norm_mod(h_ref[...], g_ref[...], sc_ref[...], sh_ref[...]).astype(BF16)
    col = lambda j: slice(j * HEAD_DIM, (j + 1) * HEAD_DIM)
    w_q = w_ref.shape[1] - 2 * ATT_KV_HEADS * HEAD_DIM
    if n_q:
        pq = _dot(a, w_ref[:, :w_q])
        for hd in range(n_q):
            t = _head_norm(pq[:, col(hd)], gq_ref[...])
            if rope:
                t = _rope(t, cos_ref[...], sin_ref[...])
            q_ref[hd] = (t * Q_SCALE).astype(BF16)
    pkv = _dot(a, w_ref[:, w_q:])
    for hd in range(ATT_KV_HEADS):
        t = _head_norm(pkv[:, col(hd)], gk_ref[...])
        if rope:
            t = _rope(t, cos_ref[...], sin_ref[...])
        k_ref[hd] = t.astype(BF16)
        v_ref[hd] = pkv[:, col(ATT_KV_HEADS + hd)].astype(BF16)


def _qkv(h, sc, sh, g, w, gq, gk, cosf=None, sinf=None, want_q=True):
    b, l, d = h.shape
    tm = _row_tile(l, 512)
    n_heads = (w.shape[1] - 2 * ATT_KV_HEADS * HEAD_DIM) // HEAD_DIM
    n_q = n_heads if want_q else 0
    rope = cosf is not None
    per_batch = sc.shape[0] == b and b > 1
    mod_idx = (lambda bi, i: (bi, 0, 0)) if per_batch else (lambda bi, i: (0, 0, 0))
    in_specs = [
        pl.BlockSpec((None, tm, d), lambda bi, i: (bi, i, 0)),
        pl.BlockSpec((None, 1, d), mod_idx), pl.BlockSpec((None, 1, d), mod_idx),
        _const_spec(g.shape), _const_spec(w.shape), _const_spec(gq.shape), _const_spec(gk.shape),
    ]
    args = [h, sc, sh, g, w, gq, gk]
    if rope:
        in_specs += [pl.BlockSpec((tm, HEAD_DIM), lambda bi, i: (i, 0))] * 2
        args += [cosf, sinf]
    heads = lambda n: pl.BlockSpec((None, n, tm, HEAD_DIM), lambda bi, i: (bi, 0, i, 0))
    shape = lambda n: jax.ShapeDtypeStruct((b, n, l, HEAD_DIM), BF16)
    out_specs = [heads(ATT_KV_HEADS), heads(ATT_KV_HEADS)]
    out_shape = [shape(ATT_KV_HEADS), shape(ATT_KV_HEADS)]
    if n_q:
        out_specs = [heads(n_q)] + out_specs
        out_shape = [shape(n_q)] + out_shape
    return pl.pallas_call(
        functools.partial(_qkv_kernel, n_q=n_q, rope=rope),
        grid=(b, l // tm),
        in_specs=in_specs,
        out_specs=out_specs,
        out_shape=out_shape,
        compiler_params=_params("parallel", "parallel"),
        name="qkv_rope" if rope else "kv_ctx",
    )(*args)


def _attn_kernel(q_ref, k_ref, v_ref, o_ref, m_ref, l_ref, acc_ref, al_ref, s_ref, p_ref, *, tk):
    grp, tq, hd = q_ref.shape
    nlt = tk // LANE
    nk = k_ref.shape[0] // tk
    assert grp % 2 == 0 and hd == LANE
    m_ref[...] = jnp.full(m_ref.shape, -jnp.inf, F32)
    l_ref[...] = jnp.zeros(l_ref.shape, F32)
    acc_ref[...] = jnp.zeros(acc_ref.shape, F32)
    al_ref[...] = jnp.ones(al_ref.shape, F32)
    p_ref[(grp - 1) % 2] = jnp.zeros(p_ref.shape[1:], BF16)

    def chunk(ref, j):
        start = pl.multiple_of(jnp.clip(j, 0, nk - 1) * tk, tk)
        return ref[pl.ds(start, tk), :]

    def scores(g, j, slot):
        s_ref[slot] = lax.dot_general(q_ref[g], chunk(k_ref, j), _NT, preferred_element_type=F32)

    def softmax(g, slot):
        mx = s_ref[slot, :, :LANE]
        for t in range(1, nlt):
            mx = jnp.maximum(mx, s_ref[slot, :, t * LANE:(t + 1) * LANE])
        m_old = m_ref[g]
        m_new = jnp.maximum(m_old, jnp.max(mx, axis=-1, keepdims=True))
        al_ref[g] = jnp.exp2(m_old - m_new)
        m_ref[g] = m_new
        ps = None
        for t in range(nlt):
            p = jnp.exp2(s_ref[slot, :, t * LANE:(t + 1) * LANE] - m_new)
            p_ref[slot, :, t * LANE:(t + 1) * LANE] = p.astype(BF16)
            ps = p if ps is None else ps + p
        l_ref[g] = al_ref[g] * l_ref[g] + ps

    def values(g, j, slot):
        acc_ref[g] = al_ref[g] * acc_ref[g] + _dot(p_ref[slot], chunk(v_ref, j))

    scores(0, 0, 0)

    def body(j, carry):
        for g in range(grp):
            scores((g + 1) % grp, j + (g + 1) // grp, (g + 1) % 2)
            softmax(g, g % 2)
            values((g - 1) % grp, j + (g - 1) // grp, (g - 1) % 2)
        return carry

    lax.fori_loop(0, nk, body, 0)
    values(grp - 1, nk - 1, (grp - 1) % 2)
    for g in range(grp):
        out = acc_ref[g] / jnp.sum(l_ref[g], axis=-1, keepdims=True)
        o_ref[:, g * hd:(g + 1) * hd] = out.astype(o_ref.dtype)


def _attention(q, k, v):
    b, h, s, hd = q.shape
    hkv, lk = k.shape[1], k.shape[2]
    grp = h // hkv
    tq = _row_tile(s, 256)
    tk = next(t for t in (768, 512, 256, 128) if lk % t == 0)
    kv = pl.BlockSpec((None, None, lk, hd), lambda bi, kh, i: (bi, kh, 0, 0))
    return pl.pallas_call(
        functools.partial(_attn_kernel, tk=tk),
        grid=(b, hkv, s // tq),
        in_specs=[pl.BlockSpec((None, grp, tq, hd), lambda bi, kh, i: (bi, kh, i, 0)), kv, kv],
        out_specs=pl.BlockSpec((None, tq, grp * hd), lambda bi, kh, i: (bi, i, kh)),
        out_shape=jax.ShapeDtypeStruct((b, s, h * hd), BF16),
        scratch_shapes=[pltpu.VMEM((grp, tq, LANE), F32), pltpu.VMEM((grp, tq, LANE), F32),
                        pltpu.VMEM((grp, tq, hd), F32), pltpu.VMEM((grp, tq, LANE), F32),
                        pltpu.VMEM((2, tq, tk), F32), pltpu.VMEM((2, tq, tk), BF16)],
        compiler_params=_params("parallel", "parallel", "arbitrary"),
        name="gqa_flash",
    )(q, k, v)


def _proj_res_kernel(h_ref, x_ref, w_ref, ga_ref, o_ref):
    o_ref[...] = h_ref[...] + ga_ref[...] * _dot(x_ref[...], w_ref[...])


def _proj_res(h, x, w, ga):
    b, l, d = h.shape
    tm = _row_tile(l, 512)
    row = lambda w_: pl.BlockSpec((None, tm, w_), lambda bi, i: (bi, i, 0))
    return pl.pallas_call(
        _proj_res_kernel,
        grid=(b, l // tm),
        in_specs=[row(d), row(x.shape[-1]), _const_spec(w.shape),
                  pl.BlockSpec((None, 1, d), lambda bi, i: (bi, 0, 0))],
        out_specs=row(d),
        out_shape=jax.ShapeDtypeStruct((b, l, d), F32),
        compiler_params=_params("parallel", "parallel"),
        name="att_out",
    )(h, x, w, ga)


def _rope_tables(n_tokens):
    rows = n_tokens // GRID_W
    half = HEAD_DIM // 2
    r, c = jnp.meshgrid(jnp.arange(rows), jnp.arange(GRID_W), indexing='ij')
    inv = ROPE_THETA ** (-jnp.arange(0, half, 2, dtype=F32) / half)
    ang = jnp.concatenate([r.reshape(-1, 1).astype(F32) * inv,
                           c.reshape(-1, 1).astype(F32) * inv], axis=-1)
    cos, sin = jnp.cos(ang), jnp.sin(ang)
    return jnp.concatenate([cos, cos], axis=-1), jnp.concatenate([-sin, sin], axis=-1)


def kernel(x, c, ctx, c_ctx, w_mod, b_mod, g_norm_mix, g_norm_ffn, g_norm_final, w_even_in, w_gla_gate,
           b_gla_gate, g_gla_out, w_even_out, w_qkv, g_q, g_k, w_att_out, w_ffn_up, w_ffn_conv, b_ffn_conv,
           w_ffn_down):
    bsz, seq, d = x.shape
    assert w_mod.shape[0] == 2, "two layers: one even (Fourier || GLA) and one odd (attention)"
    fw = FOURIER_GROUPS * LANE
    kw = GLA_HEADS * GLA_DK
    vw = GLA_HEADS * GLA_DV
    main_w = fw + 2 * kw + 2 * vw

    rows = -(-(bsz + 1) // 8) * 8
    cv = jnp.zeros((rows, d), F32).at[:bsz].set(c).at[bsz].set(c_ctx)
    mod = _modulation(cv, w_mod, b_mod).reshape(2, rows, N_MOD, d)
    lat = lambda i, j: mod[i, :bsz, j][:, None, :]
    cx = lambda i, j: mod[i, bsz:bsz + 1, j][:, None, :]
    row2 = lambda v: v.reshape(1, -1)

    w_in = w_even_in[0]
    wm = w_in[:, :main_w].astype(BF16)
    wz = jnp.zeros((d, LANE), F32).at[:, :GLA_GATE_RANK].set(w_in[:, main_w:]).astype(BF16)
    wg = jnp.zeros((LANE, 2 * kw), F32).at[:GLA_GATE_RANK].set(
        jnp.concatenate([w_gla_gate[0, 0], w_gla_gate[0, 1]], axis=-1)).astype(BF16)
    bg = b_gla_gate[0].reshape(1, 2 * kw)
    nch = np.arange(LANE)
    angc = 2.0 * np.pi * ((nch[:, None] * nch[None, :]) % LANE) / LANE
    cs = _table(np.concatenate([np.cos(angc), np.sin(angc)], axis=1) / math.sqrt(LANE))
    ic = np.arange(GLA_CHUNK)
    tril = jnp.asarray(ic[:, None] >= ic[None, :], BF16)
    triu = jnp.asarray(ic[:, None] <= ic[None, :], BF16)
    w_out = w_even_out[0].astype(BF16)
    gg = row2(g_gla_out[0])
    ffn_w = [(w_ffn_up[i].astype(BF16), w_ffn_conv[i], row2(b_ffn_conv[i]), w_ffn_down[i].astype(BF16))
             for i in range(2)]

    def even_layer(h, sc1, sh1, ga1, sc2, sh2, ga2, s0f, s0b):
        af, bf, q, k, v, r, cf, cb = _inproj(h, sc1, sh1, row2(g_norm_mix[0]), wm, wz, cs, wg, bg, tril, triu)
        four = _position_dft(af, bf)
        of, ob, sf, sb = _gla(q, k, v, cf, cb, s0f, s0b)
        h = _mixout(h, four, of, ob, r, gg, w_out, ga1)
        h = _conv_ffn(h, sc2, sh2, ga2, row2(g_norm_ffn[0]), *ffn_w[0])
        return h, sf, sb

    zero_state = jnp.zeros((bsz, vw, kw), F32)
    h_ctx, s_f, s_b = even_layer(ctx, cx(0, 1), cx(0, 0), cx(0, 2), cx(0, 4), cx(0, 3), cx(0, 5),
                                 zero_state, zero_state)
    h_lat, _, _ = even_layer(x, lat(0, 1), lat(0, 0), lat(0, 2), lat(0, 4), lat(0, 3), lat(0, 5), s_f, s_b)

    wq = w_qkv[0].astype(BF16)
    cosf, sinf = _rope_tables(seq)
    gq, gk, gm = row2(g_q[0]), row2(g_k[0]), row2(g_norm_mix[1])
    k_c, v_c = _qkv(h_ctx, cx(1, 1), cx(1, 0), gm, wq, gq, gk, want_q=False)
    q_l, k_l, v_l = _qkv(h_lat, lat(1, 1), lat(1, 0), gm, wq, gq, gk, cosf, sinf)
    k_all = jnp.concatenate([k_c, k_l], axis=2)
    v_all = jnp.concatenate([v_c, v_l], axis=2)
    att = _attention(q_l, k_all, v_all)
    h_lat = _proj_res(h_lat, att, w_att_out[0].astype(BF16), lat(1, 2))
    return _conv_ffn(h_lat, lat(1, 4), lat(1, 3), lat(1, 5), row2(g_norm_ffn[1]), *ffn_w[1],
                     gfin=row2(g_norm_final))
```

```python
import functools
import math

import jax
import jax.numpy as jnp
import numpy as np
from jax import lax
from jax.experimental import pallas as pl
from jax.experimental.pallas import tpu as pltpu

F32 = jnp.float32
BF16 = jnp.bfloat16

EPS = 1e-6
N_MOD = 6
LANE = 128
SUBLANE_BF16 = 16
VMEM_LIMIT = 56 * 1024 * 1024

FOURIER_GROUPS = 4
GLA_HEADS = 4
GLA_DK = 64
GLA_DV = 128
GLA_GATE_RANK = 16
GLA_GATE_TEMP = 16.0
GLA_CHUNK = 64
HEAD_DIM = 128
ATT_KV_HEADS = 2
GRID_W = 64
ROPE_THETA = 10000.0
CONV_W = 3
Q_SCALE = HEAD_DIM ** -0.5 * math.log2(math.e)
ATT_PLAIN_MAX_LOG2 = 64.0

_NT = (((1,), (1,)), ((), ()))
_TN = (((0,), (0,)), ((), ()))


def _dot(a, b):
    return jnp.dot(a, b, preferred_element_type=F32)


def _params(*sem):
    return pltpu.CompilerParams(dimension_semantics=sem, vmem_limit_bytes=VMEM_LIMIT)


def _const_spec(shape):
    nd = len(shape)
    return pl.BlockSpec(shape, lambda *_: (0,) * nd, pipeline_mode=pl.Buffered(1))


def _table(values):
    return jnp.asarray(values, F32).astype(BF16)


def _row_tile(n, cap):
    t = min(n, cap)
    assert n % t == 0, (n, t)
    return t


def _norm_mod(x, g, sc, sh):
    y = x * lax.rsqrt(jnp.mean(x * x, axis=-1, keepdims=True) + EPS)
    return y * (g * (1.0 + sc)) + sh


def _silu(x):
    return x * jax.nn.sigmoid(x)


def _mod_kernel(cv_ref, w_ref, b_ref, o_ref):
    s = _silu(cv_ref[...]).astype(BF16)
    o_ref[...] = _dot(s, w_ref[...].astype(BF16)) + b_ref[...]


def _modulation(cv, w_mod, b_mod):
    depth, d, n = w_mod.shape
    rows = cv.shape[0]
    tn = 1536
    assert n % tn == 0
    return pl.pallas_call(
        _mod_kernel,
        grid=(depth, n // tn),
        in_specs=[
            pl.BlockSpec((rows, d), lambda i, j: (0, 0)),
            pl.BlockSpec((None, d, tn), lambda i, j: (i, 0, j)),
            pl.BlockSpec((None, 1, tn), lambda i, j: (i, 0, j)),
        ],
        out_specs=pl.BlockSpec((None, rows, tn), lambda i, j: (i, 0, j)),
        out_shape=jax.ShapeDtypeStruct((depth, rows, n), F32),
        compiler_params=_params("parallel", "parallel"),
        name="adaln_mod",
    )(cv, w_mod, b_mod.reshape(depth, 1, n))


def _split3(x):
    hi = x.astype(BF16)
    r1 = x - hi.astype(F32)
    mid = r1.astype(BF16)
    lo = (r1 - mid.astype(F32)).astype(BF16)
    return hi, mid, lo


def _inproj_kernel(h_ref, sc_ref, sh_ref, g_ref, wm_ref, wz_ref, cs_ref, wg_ref, bg_ref,
                   tril_ref, triu_ref,
                   af_ref, bf_ref, q_ref, k_ref, v_ref, r_ref, cf_ref, cb_ref):
    fw = FOURIER_GROUPS * LANE
    kw = GLA_HEADS * GLA_DK
    vw = GLA_HEADS * GLA_DV
    a = _norm_mod(h_ref[...], g_ref[...], sc_ref[...], sh_ref[...]).astype(BF16)
    p = _dot(a, wm_ref[...])
    for g in range(FOURIER_GROUPS):
        ab = _dot(p[:, g * LANE:(g + 1) * LANE].astype(BF16), cs_ref[...])
        af_ref[:, g * LANE:(g + 1) * LANE] = ab[:, :LANE].astype(BF16)
        bf_ref[:, g * LANE:(g + 1) * LANE] = ab[:, LANE:].astype(BF16)
    q_ref[...] = p[:, fw:fw + kw] * (GLA_DK ** -0.5)
    k_ref[...] = p[:, fw + kw:fw + 2 * kw]
    v_ref[...] = p[:, fw + 2 * kw:fw + 2 * kw + vw].astype(BF16)
    r_ref[...] = p[:, fw + 2 * kw + vw:fw + 2 * kw + 2 * vw]
    z = _dot(a, wz_ref[...])
    zz = _dot(z.astype(BF16), wg_ref[...]) + bg_ref[...]
    loga = (jnp.minimum(zz, 0.0) - jnp.log1p(jnp.exp(-jnp.abs(zz)))) * (1.0 / GLA_GATE_TEMP)
    tril = tril_ref[...]
    triu = triu_ref[...]
    tm = h_ref.shape[0]
    for c in range(tm // GLA_CHUNK):
        rows = slice(c * GLA_CHUNK, (c + 1) * GLA_CHUNK)
        la = loga[rows]
        accf = None
        accb = None
        for term in _split3(la):
            tf = _dot(tril, term[:, :kw])
            tb = _dot(triu, term[:, kw:])
            accf = tf if accf is None else accf + tf
            accb = tb if accb is None else accb + tb
        cf_ref[rows, :] = accf
        cb_ref[rows, :] = accb


def _inproj(h, sc, sh, g, wm, wz, cs, wg, bg, tril, triu):
    b, l, d = h.shape
    tm = _row_tile(l, 512)
    per_batch = sc.shape[0] == b and b > 1
    mod_idx = (lambda bi, i: (bi, 0, 0)) if per_batch else (lambda bi, i: (0, 0, 0))
    row = lambda w: pl.BlockSpec((None, tm, w), lambda bi, i: (bi, i, 0))
    outs = [(512, BF16), (512, BF16), (256, F32), (256, F32), (512, BF16), (512, F32), (256, F32), (256, F32)]
    return pl.pallas_call(
        _inproj_kernel,
        grid=(b, l // tm),
        in_specs=[
            row(d),
            pl.BlockSpec((None, 1, d), mod_idx),
            pl.BlockSpec((None, 1, d), mod_idx),
            _const_spec(g.shape), _const_spec(wm.shape), _const_spec(wz.shape), _const_spec(cs.shape),
            _const_spec(wg.shape), _const_spec(bg.shape), _const_spec(tril.shape), _const_spec(triu.shape),
        ],
        out_specs=[row(w) for w, _ in outs],
        out_shape=[jax.ShapeDtypeStruct((b, l, w), dt) for w, dt in outs],
        compiler_params=_params("parallel", "parallel"),
        name="even_inproj",
    )(h, sc, sh, g, wm, wz, cs, wg, bg, tril, triu)


def _dft_dense_kernel(a_ref, b_ref, c_ref, s_ref, o_ref):
    o_ref[...] = (_dot(c_ref[...], a_ref[...]) - _dot(s_ref[...], b_ref[...])).astype(o_ref.dtype)


def _dft_dense(af, bf):
    b, l, w = af.shape
    n = np.arange(l)
    ang = 2.0 * np.pi * ((n[:, None] * n[None, :]) % l) / l
    c = _table(np.cos(ang) / math.sqrt(l))
    s = _table(np.sin(ang) / math.sqrt(l))
    blk = pl.BlockSpec((None, l, w), lambda bi: (bi, 0, 0))
    return pl.pallas_call(
        _dft_dense_kernel,
        grid=(b,),
        in_specs=[blk, blk, _const_spec(c.shape), _const_spec(s.shape)],
        out_specs=blk,
        out_shape=jax.ShapeDtypeStruct((b, l, w), BF16),
        compiler_params=_params("parallel"),
        name="dft_dense",
    )(af, bf, c, s)


DFT_N2 = LANE
DFT_NB = 8
DFT_KB = 8


def _dft_stage1_kernel(a_ref, b_ref, ma_ref, mb_ref, tc_ref, ts_ref, y_ref):
    n1 = a_ref.shape[0]
    y = _dot(ma_ref[...], a_ref[...]) + _dot(mb_ref[...], b_ref[...])
    for j in range(DFT_NB):
        tc = tc_ref[:, j * LANE:(j + 1) * LANE]
        ts = ts_ref[:, j * LANE:(j + 1) * LANE]
        for g in range(FOURIER_GROUPS):
            sl = slice((j * FOURIER_GROUPS + g) * LANE, (j * FOURIER_GROUPS + g + 1) * LANE)
            yr = y[:n1, sl]
            yi = y[n1:, sl]
            y_ref[:n1, sl] = (yr * tc + yi * ts).astype(BF16)
            y_ref[n1:, sl] = (yi * tc - yr * ts).astype(BF16)


def _dft_stage2_kernel(yr_ref, yi_ref, c_ref, s_ref, o_ref):
    w = yr_ref.shape[-1]
    for kk in range(DFT_KB):
        o = _dot(c_ref[...], yr_ref[kk]) + _dot(s_ref[...], yi_ref[kk])
        o_ref[:, kk * w:(kk + 1) * w] = o.astype(o_ref.dtype)


def _dft_factored(af, bf):
    b, l, w = af.shape
    n1 = l // DFT_N2
    assert l % DFT_N2 == 0 and n1 % SUBLANE_BF16 == 0 and n1 % DFT_KB == 0
    k = np.arange(n1)
    ang1 = 2.0 * np.pi * ((k[:, None] * k[None, :]) % n1) / n1
    c1, s1 = np.cos(ang1), np.sin(ang1)
    ma = _table(np.concatenate([c1, -s1], axis=0))
    mb = _table(np.concatenate([-s1, -c1], axis=0))
    n2 = np.arange(DFT_N2)
    angt = 2.0 * np.pi * (k[:, None] * n2[None, :]) / l
    scale = 1.0 / math.sqrt(l)
    tc = jnp.asarray(np.repeat(np.cos(angt) * scale, LANE, axis=1), F32)
    ts = jnp.asarray(np.repeat(np.sin(angt) * scale, LANE, axis=1), F32)
    ang2 = 2.0 * np.pi * ((n2[:, None] * n2[None, :]) % DFT_N2) / DFT_N2
    c2 = _table(np.cos(ang2))
    s2 = _table(np.sin(ang2))

    a3 = af.reshape(b, n1, DFT_N2 * w)
    b3 = bf.reshape(b, n1, DFT_N2 * w)
    in_blk = pl.BlockSpec((None, n1, DFT_NB * w), lambda bi, j: (bi, 0, j))
    y = pl.pallas_call(
        _dft_stage1_kernel,
        grid=(b, DFT_N2 // DFT_NB),
        in_specs=[
            in_blk, in_blk, _const_spec(ma.shape), _const_spec(mb.shape),
            pl.BlockSpec((n1, DFT_NB * LANE), lambda bi, j: (0, j)),
            pl.BlockSpec((n1, DFT_NB * LANE), lambda bi, j: (0, j)),
        ],
        out_specs=pl.BlockSpec((None, 2 * n1, DFT_NB * w), lambda bi, j: (bi, 0, j)),
        out_shape=jax.ShapeDtypeStruct((b, 2 * n1, DFT_N2 * w), BF16),
        compiler_params=_params("parallel", "parallel"),
        name="dft_stage1",
    )(a3, b3, ma, mb, tc, ts)

    y4 = y.reshape(b, 2 * n1, DFT_N2, w)
    nkb = n1 // DFT_KB
    out = pl.pallas_call(
        _dft_stage2_kernel,
        grid=(b, nkb),
        in_specs=[
            pl.BlockSpec((None, DFT_KB, DFT_N2, w), lambda bi, i: (bi, i, 0, 0)),
            pl.BlockSpec((None, DFT_KB, DFT_N2, w), lambda bi, i: (bi, nkb + i, 0, 0)),
            _const_spec(c2.shape), _const_spec(s2.shape),
        ],
        out_specs=pl.BlockSpec((None, DFT_N2, DFT_KB * w), lambda bi, i: (bi, 0, i)),
        out_shape=jax.ShapeDtypeStruct((b, DFT_N2, n1 * w), BF16),
        compiler_params=_params("parallel", "parallel"),
        name="dft_stage2",
    )(y4, y4, c2, s2)
    return out.reshape(b, l, w)


def _position_dft(af, bf):
    l = af.shape[1]
    if l % (DFT_N2 * SUBLANE_BF16) == 0:
        return _dft_factored(af, bf)
    return _dft_dense(af, bf)


def _gla_chunk(q, k, v, cum, s_ref, head_masks, tri_mask, state_mask, forward):
    c = q.shape[0]
    tot = cum[c - 1:c] if forward else cum[0:1]
    qe = q * jnp.exp(cum)
    ke = (k * jnp.exp(-cum)).astype(BF16)
    kd = (k * jnp.exp(tot - cum)).astype(BF16)
    q_stack = jnp.concatenate([qe * hm for hm in head_masks], axis=0).astype(BF16)
    att = lax.dot_general(q_stack, ke, _NT, preferred_element_type=F32)
    att = jnp.where(tri_mask, att, 0.0).astype(BF16)
    o_full = _dot(att, v)
    o_intra = jnp.concatenate(
        [o_full[h * c:(h + 1) * c, h * GLA_DV:(h + 1) * GLA_DV] for h in range(GLA_HEADS)], axis=1)
    s = s_ref[...]
    o_inter = lax.dot_general(qe.astype(BF16), s.astype(BF16), _NT, preferred_element_type=F32)
    upd = lax.dot_general(v, kd, _TN, preferred_element_type=F32)
    s_ref[...] = s * jnp.exp(tot) + jnp.where(state_mask, upd, 0.0)
    return o_intra + o_inter


def _gla_kernel(qf_ref, kf_ref, vf_ref, cf_ref, qb_ref, kb_ref, vb_ref, cb_ref, s0f_ref, s0b_ref,
                of_ref, ob_ref, sf_ref, sb_ref):
    t = pl.program_id(1)
    c = GLA_CHUNK
    kw = GLA_HEADS * GLA_DK
    vw = GLA_HEADS * GLA_DV

    @pl.when(t == 0)
    def _():
        sf_ref[...] = s0f_ref[...]
        sb_ref[...] = s0b_ref[...]

    lane = lax.broadcasted_iota(jnp.int32, (1, kw), 1)
    head_masks = [(lane // GLA_DK == h).astype(F32) for h in range(GLA_HEADS)]
    ri = lax.broadcasted_iota(jnp.int32, (GLA_HEADS * c, c), 0) % c
    ci = lax.broadcasted_iota(jnp.int32, (GLA_HEADS * c, c), 1)
    tril = ci <= ri
    triu = ci >= ri
    sr = lax.broadcasted_iota(jnp.int32, (vw, kw), 0) // GLA_DV
    scol = lax.broadcasted_iota(jnp.int32, (vw, kw), 1) // GLA_DK
    state_mask = sr == scol

    nc = qf_ref.shape[0] // c
    for i in range(nc):
        rf = slice(i * c, (i + 1) * c)
        of_ref[rf, :] = _gla_chunk(qf_ref[rf, :], kf_ref[rf, :], vf_ref[rf, :], cf_ref[rf, :],
                                   sf_ref, head_masks, tril, state_mask, True)
        rb = slice((nc - 1 - i) * c, (nc - i) * c)
        ob_ref[rb, :] = _gla_chunk(qb_ref[rb, :], kb_ref[rb, :], vb_ref[rb, :], cb_ref[rb, :],
                                   sb_ref, head_masks, triu, state_mask, False)


def _gla(q, k, v, cf, cb, s0f, s0b):
    b, l, kw = q.shape
    vw = v.shape[-1]
    tl = _row_tile(l, 512)
    nt = l // tl
    fwd = lambda w: pl.BlockSpec((None, tl, w), lambda bi, t: (bi, t, 0))
    bwd = lambda w: pl.BlockSpec((None, tl, w), lambda bi, t: (bi, nt - 1 - t, 0))
    st = pl.BlockSpec((None, vw, kw), lambda bi, t: (bi, 0, 0))
    return pl.pallas_call(
        _gla_kernel,
        grid=(b, nt),
        in_specs=[fwd(kw), fwd(kw), fwd(vw), fwd(kw), bwd(kw), bwd(kw), bwd(vw), bwd(kw), st, st],
        out_specs=[fwd(vw), bwd(vw), st, st],
        out_shape=[jax.ShapeDtypeStruct((b, l, vw), F32), jax.ShapeDtypeStruct((b, l, vw), F32),
                   jax.ShapeDtypeStruct((b, vw, kw), F32), jax.ShapeDtypeStruct((b, vw, kw), F32)],
        compiler_params=_params("parallel", "arbitrary"),
        name="gla_scan",
    )(q, k, v, cf, q, k, v, cb, s0f, s0b)


def _mixout_kernel(h_ref, f_ref, of_ref, ob_ref, r_ref, gg_ref, w_ref, ga_ref, o_ref):
    fw = f_ref.shape[-1]
    o = of_ref[...] + ob_ref[...]
    parts = []
    for hd in range(GLA_HEADS):
        oh = o[:, hd * GLA_DV:(hd + 1) * GLA_DV]
        parts.append(oh * lax.rsqrt(jnp.mean(oh * oh, axis=-1, keepdims=True) + EPS))
    on = jnp.concatenate(parts, axis=1) * gg_ref[...]
    on = (on * _silu(r_ref[...])).astype(BF16)
    y = _dot(f_ref[...], w_ref[:fw, :]) + _dot(on, w_ref[fw:, :])
    o_ref[...] = h_ref[...] + ga_ref[...] * y


def _mixout(h, four, of, ob, r, gg, w, ga):
    b, l, d = h.shape
    tm = _row_tile(l, 512)
    per_batch = ga.shape[0] == b and b > 1
    mod_idx = (lambda bi, i: (bi, 0, 0)) if per_batch else (lambda bi, i: (0, 0, 0))
    row = lambda w_: pl.BlockSpec((None, tm, w_), lambda bi, i: (bi, i, 0))
    return pl.pallas_call(
        _mixout_kernel,
        grid=(b, l // tm),
        in_specs=[row(d), row(four.shape[-1]), row(of.shape[-1]), row(ob.shape[-1]), row(r.shape[-1]),
                  _const_spec(gg.shape), _const_spec(w.shape), pl.BlockSpec((None, 1, d), mod_idx)],
        out_specs=row(d),
        out_shape=jax.ShapeDtypeStruct((b, l, d), F32),
        compiler_params=_params("parallel", "parallel"),
        name="even_mixout",
    )(h, four, of, ob, r, gg, w, ga)


FFN_HALO = SUBLANE_BF16
FFN_TF = 256
FFN_PSLOTS = 4
FFN_DOWN_CHUNKS = 3


def _ffn_kernel(h_ref, hp_ref, hn_ref, sc_ref, sh_ref, ga_ref, g_ref, wup_ref, wc_ref, bc_ref, wdn_ref,
                *rest, final):
    if final:
        gfin_ref, o_ref, fext_ref, act_ref, *p_ref = rest
    else:
        o_ref, fext_ref, act_ref, *p_ref = rest
    i = pl.program_id(1)
    last = pl.num_programs(1) - 1
    tm = h_ref.shape[0]
    dff = wdn_ref.shape[0]
    hal = FFN_HALO
    g, sc, sh = g_ref[...], sc_ref[...], sh_ref[...]
    h = h_ref[...]
    fp = _norm_mod(hp_ref[...], g, sc, sh) * (i > 0).astype(F32)
    fn = _norm_mod(hn_ref[...], g, sc, sh) * (i < last).astype(F32)
    fext_ref[0:hal, :] = fp.astype(BF16)
    fext_ref[hal:hal + tm, :] = _norm_mod(h, g, sc, sh).astype(BF16)
    fext_ref[hal + tm:, :] = fn.astype(BF16)
    fext = fext_ref[...]
    acc = None
    nch = dff // FFN_TF
    flushed = 0

    def cols_of(c, half):
        return slice(half * dff + c * FFN_TF, half * dff + (c + 1) * FFN_TF)

    def up(c):
        for half in range(2):
            p_ref[(2 * c + half) % FFN_PSLOTS][...] = _dot(fext, wup_ref[:, cols_of(c, half)])

    def conv(c, half):
        pr = p_ref[(2 * c + half) % FFN_PSLOTS]
        cs = cols_of(c, half)
        return (pr[pl.ds(hal - 1, tm), :] * wc_ref[0:1, cs]
                + pr[pl.ds(hal, tm), :] * wc_ref[1:2, cs]
                + pr[pl.ds(hal + 1, tm), :] * wc_ref[2:3, cs]
                + bc_ref[:, cs])

    up(0)
    for c in range(nch):
        if c + 1 < nch:
            up(c + 1)
        act_ref[:, c * FFN_TF:(c + 1) * FFN_TF] = (_silu(conv(c, 0)) * conv(c, 1)).astype(BF16)
        if (c + 1) % FFN_DOWN_CHUNKS == 0 or c == nch - 1:
            ks = slice(flushed * FFN_TF, (c + 1) * FFN_TF)
            d = _dot(act_ref[:, ks], wdn_ref[ks, :])
            acc = d if acc is None else acc + d
            flushed = c + 1
    out = h + ga_ref[...] * acc
    if final:
        out = out * lax.rsqrt(jnp.mean(out * out, axis=-1, keepdims=True) + EPS) * gfin_ref[...]
    o_ref[...] = out


def _conv_ffn(h, sc, sh, ga, g, wup, wc, bc, wdn, gfin=None):
    b, l, d = h.shape
    tm = _row_tile(l, 512)
    assert tm % FFN_HALO == 0 and wdn.shape[0] % FFN_TF == 0 and CONV_W == 3
    per_batch = ga.shape[0] == b and b > 1
    mod_idx = (lambda bi, i: (bi, 0, 0)) if per_batch else (lambda bi, i: (0, 0, 0))
    hb = tm // FFN_HALO
    nh = l // FFN_HALO
    mod = pl.BlockSpec((None, 1, d), mod_idx)
    in_specs = [
        pl.BlockSpec((None, tm, d), lambda bi, i: (bi, i, 0)),
        pl.BlockSpec((None, FFN_HALO, d), lambda bi, i: (bi, jnp.maximum(i * hb - 1, 0), 0)),
        pl.BlockSpec((None, FFN_HALO, d), lambda bi, i: (bi, jnp.minimum((i + 1) * hb, nh - 1), 0)),
        mod, mod, mod,
        _const_spec(g.shape), _const_spec(wup.shape), _const_spec(wc.shape), _const_spec(bc.shape),
        _const_spec(wdn.shape),
    ]
    args = [h, h, h, sc, sh, ga, g, wup, wc, bc, wdn]
    if gfin is not None:
        in_specs.append(_const_spec(gfin.shape))
        args.append(gfin)
    return pl.pallas_call(
        functools.partial(_ffn_kernel, final=gfin is not None),
        grid=(b, l // tm),
        in_specs=in_specs,
        out_specs=pl.BlockSpec((None, tm, d), lambda bi, i: (bi, i, 0)),
        out_shape=jax.ShapeDtypeStruct((b, l, d), F32),
        scratch_shapes=([pltpu.VMEM((tm + 2 * FFN_HALO, d), BF16), pltpu.VMEM((tm, wdn.shape[0]), BF16)]
                        + [pltpu.VMEM((tm + 2 * FFN_HALO, FFN_TF), F32)] * FFN_PSLOTS),
        compiler_params=_params("parallel", "parallel"),
        name="conv_ffn_final" if gfin is not None else "conv_ffn",
    )(*args)


def _head_norm(t, g):
    return t * lax.rsqrt(jnp.mean(t * t, axis=-1, keepdims=True) + EPS) * g


def _rope(t, cosf, sinf):
    return t * cosf + pltpu.roll(t, HEAD_DIM // 2, 1) * sinf


def _qkv_kernel(h_ref, sc_ref, sh_ref, g_ref, w_ref, gq_ref, gk_ref, *rest, n_q, rope):
    if rope:
        cos_ref, sin_ref = rest[:2]
        rest = rest[2:]
    if n_q:
        q_ref, k_ref, v_ref = rest
    else:
        k_ref, v_ref = rest
    a = _norm_mod(h_ref[...], g_ref[...], sc_ref[...], sh_ref[...]).astype(BF16)
    col = lambda j: slice(j * HEAD_DIM, (j + 1) * HEAD_DIM)
    w_q = w_ref.shape[1] - 2 * ATT_KV_HEADS * HEAD_DIM
    if n_q:
        pq = _dot(a, w_ref[:, :w_q])
        for hd in range(n_q):
            t = _head_norm(pq[:, col(hd)], gq_ref[...])
            if rope:
                t = _rope(t, cos_ref[...], sin_ref[...])
            q_ref[hd] = (t * Q_SCALE).astype(BF16)
    pkv = _dot(a, w_ref[:, w_q:])
    for hd in range(ATT_KV_HEADS):
        t = _head_norm(pkv[:, col(hd)], gk_ref[...])
        if rope:
            t = _rope(t, cos_ref[...], sin_ref[...])
        k_ref[hd] = t.astype(BF16)
        v_ref[hd] = pkv[:, col(ATT_KV_HEADS + hd)].astype(BF16)


def _qkv(h, sc, sh, g, w, gq, gk, cosf=None, sinf=None, want_q=True):
    b, l, d = h.shape
    tm = _row_tile(l, 512)
    n_heads = (w.shape[1] - 2 * ATT_KV_HEADS * HEAD_DIM) // HEAD_DIM
    n_q = n_heads if want_q else 0
    rope = cosf is not None
    per_batch = sc.shape[0] == b and b > 1
    mod_idx = (lambda bi, i: (bi, 0, 0)) if per_batch else (lambda bi, i: (0, 0, 0))
    in_specs = [
        pl.BlockSpec((None, tm, d), lambda bi, i: (bi, i, 0)),
        pl.BlockSpec((None, 1, d), mod_idx), pl.BlockSpec((None, 1, d), mod_idx),
        _const_spec(g.shape), _const_spec(w.shape), _const_spec(gq.shape), _const_spec(gk.shape),
    ]
    args = [h, sc, sh, g, w, gq, gk]
    if rope:
        in_specs += [pl.BlockSpec((tm, HEAD_DIM), lambda bi, i: (i, 0))] * 2
        args += [cosf, sinf]
    heads = lambda n: pl.BlockSpec((None, n, tm, HEAD_DIM), lambda bi, i: (bi, 0, i, 0))
    shape = lambda n: jax.ShapeDtypeStruct((b, n, l, HEAD_DIM), BF16)
    out_specs = [heads(ATT_KV_HEADS), heads(ATT_KV_HEADS)]
    out_shape = [shape(ATT_KV_HEADS), shape(ATT_KV_HEADS)]
    if n_q:
        out_specs = [heads(n_q)] + out_specs
        out_shape = [shape(n_q)] + out_shape
    return pl.pallas_call(
        functools.partial(_qkv_kernel, n_q=n_q, rope=rope),
        grid=(b, l // tm),
        in_specs=in_specs,
        out_specs=out_specs,
        out_shape=out_shape,
        compiler_params=_params("parallel", "parallel"),
        name="qkv_rope" if rope else "kv_ctx",
    )(*args)


def _attn_kernel(q_ref, k_ref, v_ref, o_ref, *scratch, tk, online):
    grp, tq, hd = q_ref.shape
    nlt = tk // LANE
    nk = k_ref.shape[0] // tk
    assert grp % 2 == 0 and hd == LANE
    l_ref, acc_ref = scratch[:grp], scratch[grp:2 * grp]
    p_ref = scratch[2 * grp:2 * grp + 2]
    if online:
        rest = scratch[2 * grp + 2:]
        m_ref, al_ref, s_ref, mx_ref = rest[:grp], rest[grp:2 * grp], rest[2 * grp:2 * grp + 2], rest[2 * grp + 2:]
    for g in range(grp):
        l_ref[g][...] = jnp.zeros(l_ref[g].shape, F32)
        acc_ref[g][...] = jnp.zeros(acc_ref[g].shape, F32)
        if online:
            m_ref[g][...] = jnp.full(m_ref[g].shape, -jnp.inf, F32)

    def chunk(ref, j):
        start = pl.multiple_of(jnp.clip(j, 0, nk - 1) * tk, tk)
        return ref[pl.ds(start, tk), :]

    def lane_tiles(x):
        return [x[:, t * LANE:(t + 1) * LANE] for t in range(nlt)]

    def scores(g, j, slot, weight=None):
        s = lax.dot_general(q_ref[g], chunk(k_ref, j), _NT, preferred_element_type=F32)
        if online:
            s_ref[slot][...] = s
            mx_ref[slot][...] = functools.reduce(jnp.maximum, lane_tiles(s))
        else:
            p = jnp.exp2(s)
            p_ref[slot][...] = p.astype(BF16)
            ps = functools.reduce(jnp.add, lane_tiles(p))
            l_ref[g][...] += ps if weight is None else ps * weight

    def softmax(g, slot):
        m_old = m_ref[g][...]
        m_new = jnp.maximum(m_old, jnp.max(mx_ref[slot][...], axis=-1, keepdims=True))
        alpha = jnp.exp2(m_old - m_new)
        al_ref[g][...] = alpha
        m_ref[g][...] = m_new
        ps = None
        for t in range(nlt):
            p = jnp.exp2(s_ref[slot][:, t * LANE:(t + 1) * LANE] - m_new)
            p_ref[slot][:, t * LANE:(t + 1) * LANE] = p.astype(BF16)
            ps = p if ps is None else ps + p
        l_ref[g][...] = alpha * l_ref[g][...] + ps

    def values(g, j, slot):
        pv = _dot(p_ref[slot][...], chunk(v_ref, j))
        if online:
            acc_ref[g][...] = al_ref[g][...] * acc_ref[g][...] + pv
        else:
            acc_ref[g][...] += pv

    scores(0, 0, 0)
    if online:
        al_ref[grp - 1][...] = jnp.ones(al_ref[grp - 1].shape, F32)
        p_ref[(grp - 1) % 2][...] = jnp.zeros(p_ref[(grp - 1) % 2].shape, BF16)

        def body(j, carry):
            for g in range(grp):
                scores((g + 1) % grp, j + (g + 1) // grp, (g + 1) % 2)
                softmax(g, g % 2)
                values((g - 1) % grp, j + (g - 1) // grp, (g - 1) % 2)
            return carry

        lax.fori_loop(0, nk, body, 0)
        values(grp - 1, nk - 1, (grp - 1) % 2)
    else:
        def body(j, carry):
            for g in range(grp):
                if g + 1 < grp:
                    scores(g + 1, j, (g + 1) % 2)
                else:
                    scores(0, j + 1, 0, weight=(j + 1 < nk).astype(F32))
                values(g, j, g % 2)
            return carry

        lax.fori_loop(0, nk, body, 0)
    for g in range(grp):
        out = acc_ref[g][...] / jnp.sum(l_ref[g][...], axis=-1, keepdims=True)
        o_ref[:, g * hd:(g + 1) * hd] = out.astype(o_ref.dtype)


def _attention(q, k, v, online):
    b, h, s, hd = q.shape
    hkv, lk = k.shape[1], k.shape[2]
    grp = h // hkv
    tq = _row_tile(s, 256)
    tk = next(t for t in (768, 512, 256, 128) if lk % t == 0)
    kv = pl.BlockSpec((None, None, lk, hd), lambda bi, kh, i: (bi, kh, 0, 0))
    scratch = [pltpu.VMEM((tq, LANE), F32)] * (2 * grp) + [pltpu.VMEM((tq, tk), BF16)] * 2
    if online:
        scratch += ([pltpu.VMEM((tq, LANE), F32)] * (2 * grp) + [pltpu.VMEM((tq, tk), F32)] * 2
                    + [pltpu.VMEM((tq, LANE), F32)] * 2)
    return pl.pallas_call(
        functools.partial(_attn_kernel, tk=tk, online=online),
        grid=(b, hkv, s // tq),
        in_specs=[pl.BlockSpec((None, grp, tq, hd), lambda bi, kh, i: (bi, kh, i, 0)), kv, kv],
        out_specs=pl.BlockSpec((None, tq, grp * hd), lambda bi, kh, i: (bi, i, kh)),
        out_shape=jax.ShapeDtypeStruct((b, s, h * hd), BF16),
        scratch_shapes=scratch,
        compiler_params=_params("parallel", "parallel", "arbitrary"),
        name="gqa_flash_online" if online else "gqa_flash",
    )(q, k, v)


def _proj_res_kernel(h_ref, x_ref, w_ref, ga_ref, o_ref):
    o_ref[...] = h_ref[...] + ga_ref[...] * _dot(x_ref[...], w_ref[...])


def _proj_res(h, x, w, ga):
    b, l, d = h.shape
    tm = _row_tile(l, 512)
    row = lambda w_: pl.BlockSpec((None, tm, w_), lambda bi, i: (bi, i, 0))
    return pl.pallas_call(
        _proj_res_kernel,
        grid=(b, l // tm),
        in_specs=[row(d), row(x.shape[-1]), _const_spec(w.shape),
                  pl.BlockSpec((None, 1, d), lambda bi, i: (bi, 0, 0))],
        out_specs=row(d),
        out_shape=jax.ShapeDtypeStruct((b, l, d), F32),
        compiler_params=_params("parallel", "parallel"),
        name="att_out",
    )(h, x, w, ga)


def _rope_tables(n_tokens):
    rows = n_tokens // GRID_W
    half = HEAD_DIM // 2
    r, c = jnp.meshgrid(jnp.arange(rows), jnp.arange(GRID_W), indexing='ij')
    inv = ROPE_THETA ** (-jnp.arange(0, half, 2, dtype=F32) / half)
    ang = jnp.concatenate([r.reshape(-1, 1).astype(F32) * inv,
                           c.reshape(-1, 1).astype(F32) * inv], axis=-1)
    cos, sin = jnp.cos(ang), jnp.sin(ang)
    return jnp.concatenate([cos, cos], axis=-1), jnp.concatenate([-sin, sin], axis=-1)


def kernel(x, c, ctx, c_ctx, w_mod, b_mod, g_norm_mix, g_norm_ffn, g_norm_final, w_even_in, w_gla_gate,
           b_gla_gate, g_gla_out, w_even_out, w_qkv, g_q, g_k, w_att_out, w_ffn_up, w_ffn_conv, b_ffn_conv,
           w_ffn_down):
    bsz, seq, d = x.shape
    assert w_mod.shape[0] == 2, "two layers: one even (Fourier || GLA) and one odd (attention)"
    fw = FOURIER_GROUPS * LANE
    kw = GLA_HEADS * GLA_DK
    vw = GLA_HEADS * GLA_DV
    main_w = fw + 2 * kw + 2 * vw

    rows = -(-(bsz + 1) // 8) * 8
    cv = jnp.zeros((rows, d), F32).at[:bsz].set(c).at[bsz].set(c_ctx)
    mod = _modulation(cv, w_mod, b_mod).reshape(2, rows, N_MOD, d)
    lat = lambda i, j: mod[i, :bsz, j][:, None, :]
    cx = lambda i, j: mod[i, bsz:bsz + 1, j][:, None, :]
    row2 = lambda v: v.reshape(1, -1)

    w_in = w_even_in[0]
    wm = w_in[:, :main_w].astype(BF16)
    wz = jnp.zeros((d, LANE), F32).at[:, :GLA_GATE_RANK].set(w_in[:, main_w:]).astype(BF16)
    wg = jnp.zeros((LANE, 2 * kw), F32).at[:GLA_GATE_RANK].set(
        jnp.concatenate([w_gla_gate[0, 0], w_gla_gate[0, 1]], axis=-1)).astype(BF16)
    bg = b_gla_gate[0].reshape(1, 2 * kw)
    nch = np.arange(LANE)
    angc = 2.0 * np.pi * ((nch[:, None] * nch[None, :]) % LANE) / LANE
    cs = _table(np.concatenate([np.cos(angc), np.sin(angc)], axis=1) / math.sqrt(LANE))
    ic = np.arange(GLA_CHUNK)
    tril = jnp.asarray(ic[:, None] >= ic[None, :], BF16)
    triu = jnp.asarray(ic[:, None] <= ic[None, :], BF16)
    w_out = w_even_out[0].astype(BF16)
    gg = row2(g_gla_out[0])
    ffn_w = [(w_ffn_up[i].astype(BF16), w_ffn_conv[i], row2(b_ffn_conv[i]), w_ffn_down[i].astype(BF16))
             for i in range(2)]

    def even_layer(h, sc1, sh1, ga1, sc2, sh2, ga2, s0f, s0b):
        af, bf, q, k, v, r, cf, cb = _inproj(h, sc1, sh1, row2(g_norm_mix[0]), wm, wz, cs, wg, bg, tril, triu)
        four = _position_dft(af, bf)
        of, ob, sf, sb = _gla(q, k, v, cf, cb, s0f, s0b)
        h = _mixout(h, four, of, ob, r, gg, w_out, ga1)
        h = _conv_ffn(h, sc2, sh2, ga2, row2(g_norm_ffn[0]), *ffn_w[0])
        return h, sf, sb

    zero_state = jnp.zeros((bsz, vw, kw), F32)
    h_ctx, s_f, s_b = even_layer(ctx, cx(0, 1), cx(0, 0), cx(0, 2), cx(0, 4), cx(0, 3), cx(0, 5),
                                 zero_state, zero_state)
    h_lat, _, _ = even_layer(x, lat(0, 1), lat(0, 0), lat(0, 2), lat(0, 4), lat(0, 3), lat(0, 5), s_f, s_b)

    wq = w_qkv[0].astype(BF16)
    cosf, sinf = _rope_tables(seq)
    gq, gk, gm = row2(g_q[0]), row2(g_k[0]), row2(g_norm_mix[1])
    k_c, v_c = _qkv(h_ctx, cx(1, 1), cx(1, 0), gm, wq, gq, gk, want_q=False)
    q_l, k_l, v_l = _qkv(h_lat, lat(1, 1), lat(1, 0), gm, wq, gq, gk, cosf, sinf)
    k_all = jnp.concatenate([k_c, k_l], axis=2)
    v_all = jnp.concatenate([v_c, v_l], axis=2)
    score_bound = HEAD_DIM * Q_SCALE * jnp.max(jnp.abs(g_q[0])) * jnp.max(jnp.abs(g_k[0]))
    att = lax.cond(score_bound <= ATT_PLAIN_MAX_LOG2,
                   lambda: _attention(q_l, k_all, v_all, online=False),
                   lambda: _attention(q_l, k_all, v_all, online=True))
    h_lat = _proj_res(h_lat, att, w_att_out[0].astype(BF16), lat(1, 2))
    return _conv_ffn(h_lat, lat(1, 4), lat(1, 3), lat(1, 5), row2(g_norm_ffn[1]), *ffn_w[1],
                     gfin=row2(g_norm_final))
```

```python
import functools
import math

import jax
import jax.numpy as jnp
import numpy as np
from jax import lax
from jax.experimental import pallas as pl
from jax.experimental.pallas import tpu as pltpu

F32 = jnp.float32
BF16 = jnp.bfloat16

EPS = 1e-6
N_MOD = 6
LANE = 128
SUBLANE_BF16 = 16
VMEM_LIMIT = 56 * 1024 * 1024

FOURIER_GROUPS = 4
GLA_HEADS = 4
GLA_DK = 64
GLA_DV = 128
GLA_GATE_RANK = 16
GLA_GATE_TEMP = 16.0
GLA_CHUNK = 64
HEAD_DIM = 128
ATT_KV_HEADS = 2
GRID_W = 64
ROPE_THETA = 10000.0
CONV_W = 3
Q_SCALE = HEAD_DIM ** -0.5 * math.log2(math.e)
ATT_PLAIN_MAX_LOG2 = 64.0

_NT = (((1,), (1,)), ((), ()))
_TN = (((0,), (0,)), ((), ()))


def _dot(a, b):
    return jnp.dot(a, b, preferred_element_type=F32)


def _params(*sem):
    return pltpu.CompilerParams(dimension_semantics=sem, vmem_limit_bytes=VMEM_LIMIT)


def _const_spec(shape):
    nd = len(shape)
    return pl.BlockSpec(shape, lambda *_: (0,) * nd, pipeline_mode=pl.Buffered(1))


def _table(values):
    return jnp.asarray(values, F32).astype(BF16)


def _row_tile(n, cap):
    t = min(n, cap)
    assert n % t == 0, (n, t)
    return t


def _norm_mod(x, g, sc, sh):
    y = x * lax.rsqrt(jnp.mean(x * x, axis=-1, keepdims=True) + EPS)
    return y * (g * (1.0 + sc)) + sh


def _silu(x):
    return x * jax.nn.sigmoid(x)


def _mod_kernel(cv_ref, w_ref, b_ref, o_ref):
    s = _silu(cv_ref[...]).astype(BF16)
    o_ref[...] = _dot(s, w_ref[...].astype(BF16)) + b_ref[...]


def _modulation(cv, w_mod, b_mod):
    depth, d, n = w_mod.shape
    rows = cv.shape[0]
    tn = 1536
    assert n % tn == 0
    return pl.pallas_call(
        _mod_kernel,
        grid=(depth, n // tn),
        in_specs=[
            pl.BlockSpec((rows, d), lambda i, j: (0, 0)),
            pl.BlockSpec((None, d, tn), lambda i, j: (i, 0, j)),
            pl.BlockSpec((None, 1, tn), lambda i, j: (i, 0, j)),
        ],
        out_specs=pl.BlockSpec((None, rows, tn), lambda i, j: (i, 0, j)),
        out_shape=jax.ShapeDtypeStruct((depth, rows, n), F32),
        compiler_params=_params("parallel", "parallel"),
        name="adaln_mod",
    )(cv, w_mod, b_mod.reshape(depth, 1, n))


def _split3(x):
    hi = x.astype(BF16)
    r1 = x - hi.astype(F32)
    mid = r1.astype(BF16)
    lo = (r1 - mid.astype(F32)).astype(BF16)
    return hi, mid, lo


def _inproj_kernel(h_ref, sc_ref, sh_ref, g_ref, wm_ref, wz_ref, cs_ref, wg_ref, bg_ref,
                   tril_ref, triu_ref,
                   af_ref, bf_ref, q_ref, k_ref, v_ref, r_ref, cf_ref, cb_ref):
    fw = FOURIER_GROUPS * LANE
    kw = GLA_HEADS * GLA_DK
    vw = GLA_HEADS * GLA_DV
    a = _norm_mod(h_ref[...], g_ref[...], sc_ref[...], sh_ref[...]).astype(BF16)
    p = _dot(a, wm_ref[...])
    for g in range(FOURIER_GROUPS):
        ab = _dot(p[:, g * LANE:(g + 1) * LANE].astype(BF16), cs_ref[...])
        af_ref[:, g * LANE:(g + 1) * LANE] = ab[:, :LANE].astype(BF16)
        bf_ref[:, g * LANE:(g + 1) * LANE] = ab[:, LANE:].astype(BF16)
    q_ref[...] = p[:, fw:fw + kw] * (GLA_DK ** -0.5)
    k_ref[...] = p[:, fw + kw:fw + 2 * kw]
    v_ref[...] = p[:, fw + 2 * kw:fw + 2 * kw + vw].astype(BF16)
    r_ref[...] = p[:, fw + 2 * kw + vw:fw + 2 * kw + 2 * vw]
    z = _dot(a, wz_ref[...])
    zz = _dot(z.astype(BF16), wg_ref[...]) + bg_ref[...]
    loga = (jnp.minimum(zz, 0.0) - jnp.log1p(jnp.exp(-jnp.abs(zz)))) * (1.0 / GLA_GATE_TEMP)
    tril = tril_ref[...]
    triu = triu_ref[...]
    tm = h_ref.shape[0]
    for c in range(tm // GLA_CHUNK):
        rows = slice(c * GLA_CHUNK, (c + 1) * GLA_CHUNK)
        la = loga[rows]
        accf = None
        accb = None
        for term in _split3(la):
            tf = _dot(tril, term[:, :kw])
            tb = _dot(triu, term[:, kw:])
            accf = tf if accf is None else accf + tf
            accb = tb if accb is None else accb + tb
        cf_ref[rows, :] = accf
        cb_ref[rows, :] = accb


def _inproj(h, sc, sh, g, wm, wz, cs, wg, bg, tril, triu):
    b, l, d = h.shape
    tm = _row_tile(l, 512)
    per_batch = sc.shape[0] == b and b > 1
    mod_idx = (lambda bi, i: (bi, 0, 0)) if per_batch else (lambda bi, i: (0, 0, 0))
    row = lambda w: pl.BlockSpec((None, tm, w), lambda bi, i: (bi, i, 0))
    outs = [(512, BF16), (512, BF16), (256, F32), (256, F32), (512, BF16), (512, F32), (256, F32), (256, F32)]
    return pl.pallas_call(
        _inproj_kernel,
        grid=(b, l // tm),
        in_specs=[
            row(d),
            pl.BlockSpec((None, 1, d), mod_idx),
            pl.BlockSpec((None, 1, d), mod_idx),
            _const_spec(g.shape), _const_spec(wm.shape), _const_spec(wz.shape), _const_spec(cs.shape),
            _const_spec(wg.shape), _const_spec(bg.shape), _const_spec(tril.shape), _const_spec(triu.shape),
        ],
        out_specs=[row(w) for w, _ in outs],
        out_shape=[jax.ShapeDtypeStruct((b, l, w), dt) for w, dt in outs],
        compiler_params=_params("parallel", "parallel"),
        name="even_inproj",
    )(h, sc, sh, g, wm, wz, cs, wg, bg, tril, triu)


def _dft_dense_kernel(a_ref, b_ref, c_ref, s_ref, o_ref):
    o_ref[...] = (_dot(c_ref[...], a_ref[...]) - _dot(s_ref[...], b_ref[...])).astype(o_ref.dtype)


def _dft_dense(af, bf):
    b, l, w = af.shape
    n = np.arange(l)
    ang = 2.0 * np.pi * ((n[:, None] * n[None, :]) % l) / l
    c = _table(np.cos(ang) / math.sqrt(l))
    s = _table(np.sin(ang) / math.sqrt(l))
    blk = pl.BlockSpec((None, l, w), lambda bi: (bi, 0, 0))
    return pl.pallas_call(
        _dft_dense_kernel,
        grid=(b,),
        in_specs=[blk, blk, _const_spec(c.shape), _const_spec(s.shape)],
        out_specs=blk,
        out_shape=jax.ShapeDtypeStruct((b, l, w), BF16),
        compiler_params=_params("parallel"),
        name="dft_dense",
    )(af, bf, c, s)


DFT_N2 = LANE
DFT_NB = 8
DFT_KB = 8


def _dft_stage1_kernel(a_ref, b_ref, ma_ref, mb_ref, tc_ref, ts_ref, y_ref):
    n1 = a_ref.shape[0]
    y = _dot(ma_ref[...], a_ref[...]) + _dot(mb_ref[...], b_ref[...])
    for j in range(DFT_NB):
        tc = tc_ref[:, j * LANE:(j + 1) * LANE]
        ts = ts_ref[:, j * LANE:(j + 1) * LANE]
        for g in range(FOURIER_GROUPS):
            sl = slice((j * FOURIER_GROUPS + g) * LANE, (j * FOURIER_GROUPS + g + 1) * LANE)
            yr = y[:n1, sl]
            yi = y[n1:, sl]
            y_ref[:n1, sl] = (yr * tc + yi * ts).astype(BF16)
            y_ref[n1:, sl] = (yi * tc - yr * ts).astype(BF16)


def _dft_stage2_kernel(yr_ref, yi_ref, c_ref, s_ref, o_ref):
    w = yr_ref.shape[-1]
    for kk in range(DFT_KB):
        o = _dot(c_ref[...], yr_ref[kk]) + _dot(s_ref[...], yi_ref[kk])
        o_ref[:, kk * w:(kk + 1) * w] = o.astype(o_ref.dtype)


def _dft_factored(af, bf):
    b, l, w = af.shape
    n1 = l // DFT_N2
    assert l % DFT_N2 == 0 and n1 % SUBLANE_BF16 == 0 and n1 % DFT_KB == 0
    k = np.arange(n1)
    ang1 = 2.0 * np.pi * ((k[:, None] * k[None, :]) % n1) / n1
    c1, s1 = np.cos(ang1), np.sin(ang1)
    ma = _table(np.concatenate([c1, -s1], axis=0))
    mb = _table(np.concatenate([-s1, -c1], axis=0))
    n2 = np.arange(DFT_N2)
    angt = 2.0 * np.pi * (k[:, None] * n2[None, :]) / l
    scale = 1.0 / math.sqrt(l)
    tc = jnp.asarray(np.repeat(np.cos(angt) * scale, LANE, axis=1), F32)
    ts = jnp.asarray(np.repeat(np.sin(angt) * scale, LANE, axis=1), F32)
    ang2 = 2.0 * np.pi * ((n2[:, None] * n2[None, :]) % DFT_N2) / DFT_N2
    c2 = _table(np.cos(ang2))
    s2 = _table(np.sin(ang2))

    a3 = af.reshape(b, n1, DFT_N2 * w)
    b3 = bf.reshape(b, n1, DFT_N2 * w)
    in_blk = pl.BlockSpec((None, n1, DFT_NB * w), lambda bi, j: (bi, 0, j))
    y = pl.pallas_call(
        _dft_stage1_kernel,
        grid=(b, DFT_N2 // DFT_NB),
        in_specs=[
            in_blk, in_blk, _const_spec(ma.shape), _const_spec(mb.shape),
            pl.BlockSpec((n1, DFT_NB * LANE), lambda bi, j: (0, j)),
            pl.BlockSpec((n1, DFT_NB * LANE), lambda bi, j: (0, j)),
        ],
        out_specs=pl.BlockSpec((None, 2 * n1, DFT_NB * w), lambda bi, j: (bi, 0, j)),
        out_shape=jax.ShapeDtypeStruct((b, 2 * n1, DFT_N2 * w), BF16),
        compiler_params=_params("parallel", "parallel"),
        name="dft_stage1",
    )(a3, b3, ma, mb, tc, ts)

    y4 = y.reshape(b, 2 * n1, DFT_N2, w)
    nkb = n1 // DFT_KB
    out = pl.pallas_call(
        _dft_stage2_kernel,
        grid=(b, nkb),
        in_specs=[
            pl.BlockSpec((None, DFT_KB, DFT_N2, w), lambda bi, i: (bi, i, 0, 0)),
            pl.BlockSpec((None, DFT_KB, DFT_N2, w), lambda bi, i: (bi, nkb + i, 0, 0)),
            _const_spec(c2.shape), _const_spec(s2.shape),
        ],
        out_specs=pl.BlockSpec((None, DFT_N2, DFT_KB * w), lambda bi, i: (bi, 0, i)),
        out_shape=jax.ShapeDtypeStruct((b, DFT_N2, n1 * w), BF16),
        compiler_params=_params("parallel", "parallel"),
        name="dft_stage2",
    )(y4, y4, c2, s2)
    return out.reshape(b, l, w)


def _position_dft(af, bf):
    l = af.shape[1]
    if l % (DFT_N2 * SUBLANE_BF16) == 0:
        return _dft_factored(af, bf)
    return _dft_dense(af, bf)


GLA_FACTOR_MAX_DECAY = 80.0


def _gla_chunk(q_ref, k_ref, v_ref, c_ref, r0, s_ref, head_masks, tri_mask, state_mask, pair_sum, forward,
               factored):
    c = GLA_CHUNK
    rows = pl.ds(r0, c)
    q, k, v, cum = q_ref[rows, :], k_ref[rows, :], v_ref[rows, :], c_ref[rows, :]
    tot = cum[c - 1:c] if forward else cum[0:1]
    qe = q * jnp.exp(cum)
    kd = (k * jnp.exp(tot - cum)).astype(BF16)
    if factored:
        ke = (k * jnp.exp(-cum)).astype(BF16)
        q_stack = jnp.concatenate([qe * hm for hm in head_masks], axis=0).astype(BF16)
        att = lax.dot_general(q_stack, ke, _NT, preferred_element_type=F32)
        att = jnp.where(tri_mask, att, 0.0).astype(BF16)
        o_full = _dot(att, v)
        o_intra = jnp.concatenate(
            [o_full[h * c:(h + 1) * c, h * GLA_DV:(h + 1) * GLA_DV] for h in range(GLA_HEADS)], axis=1)
    else:
        ri = lax.broadcasted_iota(jnp.int32, (c, 1), 0)
        v32 = v.astype(F32)

        def row_of(x, j):
            return jnp.sum(jnp.where(ri == j, x, 0.0), axis=0, keepdims=True)

        def key_row(j, o):
            kj, cj, vj = row_of(k, j), row_of(cum, j), row_of(v32, j)
            live = (ri >= j) if forward else (ri <= j)
            e = q * kj * jnp.exp(jnp.where(live, cum - cj, -jnp.inf))
            return o + _dot(e.astype(BF16), pair_sum) * vj

        o_intra = lax.fori_loop(0, c, key_row, jnp.zeros((c, v.shape[1]), F32))
    s = s_ref[...]
    o_inter = lax.dot_general(qe.astype(BF16), s.astype(BF16), _NT, preferred_element_type=F32)
    upd = lax.dot_general(v, kd, _TN, preferred_element_type=F32)
    s_ref[...] = s * jnp.exp(tot) + jnp.where(state_mask, upd, 0.0)
    return o_intra + o_inter


def _gla_kernel(qf_ref, kf_ref, vf_ref, cf_ref, qb_ref, kb_ref, vb_ref, cb_ref, s0f_ref, s0b_ref,
                of_ref, ob_ref, sf_ref, sb_ref, *, factored):
    t = pl.program_id(1)
    c = GLA_CHUNK
    kw = GLA_HEADS * GLA_DK
    vw = GLA_HEADS * GLA_DV

    @pl.when(t == 0)
    def _():
        sf_ref[...] = s0f_ref[...]
        sb_ref[...] = s0b_ref[...]

    lane = lax.broadcasted_iota(jnp.int32, (1, kw), 1)
    head_masks = [(lane // GLA_DK == h).astype(F32) for h in range(GLA_HEADS)]
    ri = lax.broadcasted_iota(jnp.int32, (GLA_HEADS * c, c), 0) % c
    ci = lax.broadcasted_iota(jnp.int32, (GLA_HEADS * c, c), 1)
    tril = ci <= ri
    triu = ci >= ri
    sr = lax.broadcasted_iota(jnp.int32, (vw, kw), 0) // GLA_DV
    scol = lax.broadcasted_iota(jnp.int32, (vw, kw), 1) // GLA_DK
    state_mask = sr == scol
    pr = lax.broadcasted_iota(jnp.int32, (kw, vw), 0) // GLA_DK
    pc = lax.broadcasted_iota(jnp.int32, (kw, vw), 1) // GLA_DV
    pair_sum = (pr == pc).astype(BF16)

    nc = qf_ref.shape[0] // c
    for i in range(nc):
        of_ref[pl.ds(i * c, c), :] = _gla_chunk(
            qf_ref, kf_ref, vf_ref, cf_ref, i * c, sf_ref, head_masks, tril, state_mask, pair_sum, True, factored)
        rb = (nc - 1 - i) * c
        ob_ref[pl.ds(rb, c), :] = _gla_chunk(
            qb_ref, kb_ref, vb_ref, cb_ref, rb, sb_ref, head_masks, triu, state_mask, pair_sum, False, factored)


def _gla_call(q, k, v, cf, cb, s0f, s0b, factored):
    b, l, kw = q.shape
    vw = v.shape[-1]
    tl = _row_tile(l, 512)
    nt = l // tl
    fwd = lambda w: pl.BlockSpec((None, tl, w), lambda bi, t: (bi, t, 0))
    bwd = lambda w: pl.BlockSpec((None, tl, w), lambda bi, t: (bi, nt - 1 - t, 0))
    st = pl.BlockSpec((None, vw, kw), lambda bi, t: (bi, 0, 0))
    return pl.pallas_call(
        functools.partial(_gla_kernel, factored=factored),
        grid=(b, nt),
        in_specs=[fwd(kw), fwd(kw), fwd(vw), fwd(kw), bwd(kw), bwd(kw), bwd(vw), bwd(kw), st, st],
        out_specs=[fwd(vw), bwd(vw), st, st],
        out_shape=[jax.ShapeDtypeStruct((b, l, vw), F32), jax.ShapeDtypeStruct((b, l, vw), F32),
                   jax.ShapeDtypeStruct((b, vw, kw), F32), jax.ShapeDtypeStruct((b, vw, kw), F32)],
        compiler_params=_params("parallel", "arbitrary"),
        name="gla_scan" if factored else "gla_scan_pairwise",
    )(q, k, v, cf, q, k, v, cb, s0f, s0b)


def _gla(q, k, v, cf, cb, s0f, s0b):
    c = GLA_CHUNK
    worst = jnp.minimum(jnp.min(cf[:, c - 1::c, :]), jnp.min(cb[:, ::c, :]))
    args = (q, k, v, cf, cb, s0f, s0b)
    return lax.cond(worst >= -GLA_FACTOR_MAX_DECAY,
                    lambda: _gla_call(*args, factored=True),
                    lambda: _gla_call(*args, factored=False))


def _mixout_kernel(h_ref, f_ref, of_ref, ob_ref, r_ref, gg_ref, w_ref, ga_ref, o_ref):
    fw = f_ref.shape[-1]
    o = of_ref[...] + ob_ref[...]
    parts = []
    for hd in range(GLA_HEADS):
        oh = o[:, hd * GLA_DV:(hd + 1) * GLA_DV]
        parts.append(oh * lax.rsqrt(jnp.mean(oh * oh, axis=-1, keepdims=True) + EPS))
    on = jnp.concatenate(parts, axis=1) * gg_ref[...]
    on = (on * _silu(r_ref[...])).astype(BF16)
    y = _dot(f_ref[...], w_ref[:fw, :]) + _dot(on, w_ref[fw:, :])
    o_ref[...] = h_ref[...] + ga_ref[...] * y


def _mixout(h, four, of, ob, r, gg, w, ga):
    b, l, d = h.shape
    tm = _row_tile(l, 512)
    per_batch = ga.shape[0] == b and b > 1
    mod_idx = (lambda bi, i: (bi, 0, 0)) if per_batch else (lambda bi, i: (0, 0, 0))
    row = lambda w_: pl.BlockSpec((None, tm, w_), lambda bi, i: (bi, i, 0))
    return pl.pallas_call(
        _mixout_kernel,
        grid=(b, l // tm),
        in_specs=[row(d), row(four.shape[-1]), row(of.shape[-1]), row(ob.shape[-1]), row(r.shape[-1]),
                  _const_spec(gg.shape), _const_spec(w.shape), pl.BlockSpec((None, 1, d), mod_idx)],
        out_specs=row(d),
        out_shape=jax.ShapeDtypeStruct((b, l, d), F32),
        compiler_params=_params("parallel", "parallel"),
        name="even_mixout",
    )(h, four, of, ob, r, gg, w, ga)


FFN_HALO = SUBLANE_BF16
FFN_TF = 256
FFN_PSLOTS = 4
FFN_DOWN_CHUNKS = 2


def _ffn_kernel(h_ref, hp_ref, hn_ref, sc_ref, sh_ref, ga_ref, g_ref, wup_ref, wc_ref, bc_ref, wdn_ref,
                *rest, final):
    if final:
        gfin_ref, o_ref, fext_ref, act_ref, *p_ref = rest
    else:
        o_ref, fext_ref, act_ref, *p_ref = rest
    i = pl.program_id(1)
    last = pl.num_programs(1) - 1
    tm = h_ref.shape[0]
    dff = wdn_ref.shape[0]
    hal = FFN_HALO
    g, sc, sh = g_ref[...], sc_ref[...], sh_ref[...]
    h = h_ref[...]
    fp = _norm_mod(hp_ref[...], g, sc, sh) * (i > 0).astype(F32)
    fn = _norm_mod(hn_ref[...], g, sc, sh) * (i < last).astype(F32)
    fext_ref[0:hal, :] = fp.astype(BF16)
    fext_ref[hal:hal + tm, :] = _norm_mod(h, g, sc, sh).astype(BF16)
    fext_ref[hal + tm:, :] = fn.astype(BF16)
    acc = None
    nch = dff // FFN_TF
    flushed = 0

    def cols_of(c, half):
        return slice(half * dff + c * FFN_TF, half * dff + (c + 1) * FFN_TF)

    def up(c):
        for half in range(2):
            p_ref[(2 * c + half) % FFN_PSLOTS][...] = _dot(fext_ref[...], wup_ref[:, cols_of(c, half)])

    def conv(c, half):
        pr = p_ref[(2 * c + half) % FFN_PSLOTS]
        cs = cols_of(c, half)
        return (pr[pl.ds(hal - 1, tm), :] * wc_ref[0:1, cs]
                + pr[pl.ds(hal, tm), :] * wc_ref[1:2, cs]
                + pr[pl.ds(hal + 1, tm), :] * wc_ref[2:3, cs]
                + bc_ref[:, cs])

    def down(done):
        nonlocal acc, flushed
        ks = slice(flushed * FFN_TF, done * FFN_TF)
        d = _dot(act_ref[:, ks], wdn_ref[ks, :])
        acc = d if acc is None else acc + d
        flushed = done

    up(0)
    for c in range(nch):
        if c + 1 < nch:
            up(c + 1)
        if c - flushed >= FFN_DOWN_CHUNKS:
            down(c)
        act_ref[:, c * FFN_TF:(c + 1) * FFN_TF] = (_silu(conv(c, 0)) * conv(c, 1)).astype(BF16)
    down(nch)
    out = h + ga_ref[...] * acc
    if final:
        out = out * lax.rsqrt(jnp.mean(out * out, axis=-1, keepdims=True) + EPS) * gfin_ref[...]
    o_ref[...] = out


def _conv_ffn(h, sc, sh, ga, g, wup, wc, bc, wdn, gfin=None):
    b, l, d = h.shape
    tm = _row_tile(l, 512)
    assert tm % FFN_HALO == 0 and wdn.shape[0] % FFN_TF == 0 and CONV_W == 3
    per_batch = ga.shape[0] == b and b > 1
    mod_idx = (lambda bi, i: (bi, 0, 0)) if per_batch else (lambda bi, i: (0, 0, 0))
    hb = tm // FFN_HALO
    nh = l // FFN_HALO
    mod = pl.BlockSpec((None, 1, d), mod_idx)
    in_specs = [
        pl.BlockSpec((None, tm, d), lambda bi, i: (bi, i, 0)),
        pl.BlockSpec((None, FFN_HALO, d), lambda bi, i: (bi, jnp.maximum(i * hb - 1, 0), 0)),
        pl.BlockSpec((None, FFN_HALO, d), lambda bi, i: (bi, jnp.minimum((i + 1) * hb, nh - 1), 0)),
        mod, mod, mod,
        _const_spec(g.shape), _const_spec(wup.shape), _const_spec(wc.shape), _const_spec(bc.shape),
        _const_spec(wdn.shape),
    ]
    args = [h, h, h, sc, sh, ga, g, wup, wc, bc, wdn]
    if gfin is not None:
        in_specs.append(_const_spec(gfin.shape))
        args.append(gfin)
    return pl.pallas_call(
        functools.partial(_ffn_kernel, final=gfin is not None),
        grid=(b, l // tm),
        in_specs=in_specs,
        out_specs=pl.BlockSpec((None, tm, d), lambda bi, i: (bi, i, 0)),
        out_shape=jax.ShapeDtypeStruct((b, l, d), F32),
        scratch_shapes=([pltpu.VMEM((tm + 2 * FFN_HALO, d), BF16), pltpu.VMEM((tm, wdn.shape[0]), BF16)]
                        + [pltpu.VMEM((tm + 2 * FFN_HALO, FFN_TF), F32)] * FFN_PSLOTS),
        compiler_params=_params("parallel", "parallel"),
        name="conv_ffn_final" if gfin is not None else "conv_ffn",
    )(*args)


def _head_norm(t, g):
    return t * lax.rsqrt(jnp.mean(t * t, axis=-1, keepdims=True) + EPS) * g


def _rope(t, cosf, sinf):
    return t * cosf + pltpu.roll(t, HEAD_DIM // 2, 1) * sinf


def _qkv_kernel(h_ref, sc_ref, sh_ref, g_ref, w_ref, gq_ref, gk_ref, *rest, n_q, rope):
    if rope:
        cos_ref, sin_ref = rest[:2]
        rest = rest[2:]
    if n_q:
        q_ref, k_ref, v_ref = rest
    else:
        k_ref, v_ref = rest
    a = _norm_mod(h_ref[...], g_ref[...], sc_ref[...], sh_ref[...]).astype(BF16)
    col = lambda j: slice(j * HEAD_DIM, (j + 1) * HEAD_DIM)
    w_q = w_ref.shape[1] - 2 * ATT_KV_HEADS * HEAD_DIM
    if n_q:
        pq = _dot(a, w_ref[:, :w_q])
        for hd in range(n_q):
            t = _head_norm(pq[:, col(hd)], gq_ref[...])
            if rope:
                t = _rope(t, cos_ref[...], sin_ref[...])
            q_ref[hd] = (t * Q_SCALE).astype(BF16)
    pkv = _dot(a, w_ref[:, w_q:])
    for hd in range(ATT_KV_HEADS):
        t = _head_norm(pkv[:, col(hd)], gk_ref[...])
        if rope:
            t = _rope(t, cos_ref[...], sin_ref[...])
        k_ref[hd] = t.astype(BF16)
        v_ref[hd] = pkv[:, col(ATT_KV_HEADS + hd)].astype(BF16)


def _qkv(h, sc, sh, g, w, gq, gk, cosf=None, sinf=None, want_q=True):
    b, l, d = h.shape
    tm = _row_tile(l, 512)
    n_heads = (w.shape[1] - 2 * ATT_KV_HEADS * HEAD_DIM) // HEAD_DIM
    n_q = n_heads if want_q else 0
    rope = cosf is not None
    per_batch = sc.shape[0] == b and b > 1
    mod_idx = (lambda bi, i: (bi, 0, 0)) if per_batch else (lambda bi, i: (0, 0, 0))
    in_specs = [
        pl.BlockSpec((None, tm, d), lambda bi, i: (bi, i, 0)),
        pl.BlockSpec((None, 1, d), mod_idx), pl.BlockSpec((None, 1, d), mod_idx),
        _const_spec(g.shape), _const_spec(w.shape), _const_spec(gq.shape), _const_spec(gk.shape),
    ]
    args = [h, sc, sh, g, w, gq, gk]
    if rope:
        in_specs += [pl.BlockSpec((tm, HEAD_DIM), lambda bi, i: (i, 0))] * 2
        args += [cosf, sinf]
    heads = lambda n: pl.BlockSpec((None, n, tm, HEAD_DIM), lambda bi, i: (bi, 0, i, 0))
    shape = lambda n: jax.ShapeDtypeStruct((b, n, l, HEAD_DIM), BF16)
    out_specs = [heads(ATT_KV_HEADS), heads(ATT_KV_HEADS)]
    out_shape = [shape(ATT_KV_HEADS), shape(ATT_KV_HEADS)]
    if n_q:
        out_specs = [heads(n_q)] + out_specs
        out_shape = [shape(n_q)] + out_shape
    return pl.pallas_call(
        functools.partial(_qkv_kernel, n_q=n_q, rope=rope),
        grid=(b, l // tm),
        in_specs=in_specs,
        out_specs=out_specs,
        out_shape=out_shape,
        compiler_params=_params("parallel", "parallel"),
        name="qkv_rope" if rope else "kv_ctx",
    )(*args)


def _attn_kernel(q_ref, k_ref, v_ref, o_ref, *scratch, tk, online):
    grp, tq, hd = q_ref.shape
    nlt = tk // LANE
    nk = k_ref.shape[0] // tk
    assert grp % 2 == 0 and hd == LANE
    l_ref, acc_ref = scratch[:grp], scratch[grp:2 * grp]
    p_ref = scratch[2 * grp:2 * grp + 2]
    if online:
        rest = scratch[2 * grp + 2:]
        m_ref, al_ref, s_ref, mx_ref = rest[:grp], rest[grp:2 * grp], rest[2 * grp:2 * grp + 2], rest[2 * grp + 2:]
    for g in range(grp):
        l_ref[g][...] = jnp.zeros(l_ref[g].shape, F32)
        acc_ref[g][...] = jnp.zeros(acc_ref[g].shape, F32)
        if online:
            m_ref[g][...] = jnp.full(m_ref[g].shape, -jnp.inf, F32)

    def chunk(ref, j):
        if isinstance(j, int):
            return ref[j * tk:(j + 1) * tk, :]
        start = pl.multiple_of(jnp.clip(j, 0, nk - 1) * tk, tk)
        return ref[pl.ds(start, tk), :]

    def lane_tiles(x):
        return [x[:, t * LANE:(t + 1) * LANE] for t in range(nlt)]

    def scores(g, j, slot):
        s = lax.dot_general(q_ref[g], chunk(k_ref, j), _NT, preferred_element_type=F32)
        if online:
            s_ref[slot][...] = s
            mx_ref[slot][...] = functools.reduce(jnp.maximum, lane_tiles(s))
        else:
            p = jnp.exp2(s)
            p_ref[slot][...] = p.astype(BF16)
            l_ref[g][...] += functools.reduce(jnp.add, lane_tiles(p))

    def softmax(g, slot):
        m_old = m_ref[g][...]
        m_new = jnp.maximum(m_old, jnp.max(mx_ref[slot][...], axis=-1, keepdims=True))
        alpha = jnp.exp2(m_old - m_new)
        al_ref[g][...] = alpha
        m_ref[g][...] = m_new
        ps = None
        for t in range(nlt):
            p = jnp.exp2(s_ref[slot][:, t * LANE:(t + 1) * LANE] - m_new)
            p_ref[slot][:, t * LANE:(t + 1) * LANE] = p.astype(BF16)
            ps = p if ps is None else ps + p
        l_ref[g][...] = alpha * l_ref[g][...] + ps

    def values(g, j, slot):
        pv = _dot(p_ref[slot][...], chunk(v_ref, j))
        if online:
            acc_ref[g][...] = al_ref[g][...] * acc_ref[g][...] + pv
        else:
            acc_ref[g][...] += pv

    scores(0, 0, 0)
    if online:
        al_ref[grp - 1][...] = jnp.ones(al_ref[grp - 1].shape, F32)
        p_ref[(grp - 1) % 2][...] = jnp.zeros(p_ref[(grp - 1) % 2].shape, BF16)

        def body(j, carry):
            for g in range(grp):
                scores((g + 1) % grp, j + (g + 1) // grp, (g + 1) % 2)
                softmax(g, g % 2)
                values((g - 1) % grp, j + (g - 1) // grp, (g - 1) % 2)
            return carry

        lax.fori_loop(0, nk, body, 0)
        values(grp - 1, nk - 1, (grp - 1) % 2)
    else:
        for j in range(nk):
            for g in range(grp):
                if g + 1 < grp:
                    scores(g + 1, j, (g + 1) % 2)
                elif j + 1 < nk:
                    scores(0, j + 1, 0)
                values(g, j, g % 2)
    for g in range(grp):
        out = acc_ref[g][...] / jnp.sum(l_ref[g][...], axis=-1, keepdims=True)
        o_ref[:, g * hd:(g + 1) * hd] = out.astype(o_ref.dtype)


def _attention(q, k, v, online):
    b, h, s, hd = q.shape
    hkv, lk = k.shape[1], k.shape[2]
    grp = h // hkv
    tq = _row_tile(s, 256)
    tk = next(t for t in (768, 512, 256, 128) if lk % t == 0)
    kv = pl.BlockSpec((None, None, lk, hd), lambda bi, kh, i: (bi, kh, 0, 0))
    scratch = [pltpu.VMEM((tq, LANE), F32)] * (2 * grp) + [pltpu.VMEM((tq, tk), BF16)] * 2
    if online:
        scratch += ([pltpu.VMEM((tq, LANE), F32)] * (2 * grp) + [pltpu.VMEM((tq, tk), F32)] * 2
                    + [pltpu.VMEM((tq, LANE), F32)] * 2)
    return pl.pallas_call(
        functools.partial(_attn_kernel, tk=tk, online=online),
        grid=(b, hkv, s // tq),
        in_specs=[pl.BlockSpec((None, grp, tq, hd), lambda bi, kh, i: (bi, kh, i, 0)), kv, kv],
        out_specs=pl.BlockSpec((None, tq, grp * hd), lambda bi, kh, i: (bi, i, kh)),
        out_shape=jax.ShapeDtypeStruct((b, s, h * hd), BF16),
        scratch_shapes=scratch,
        compiler_params=_params("parallel", "parallel", "arbitrary"),
        name="gqa_flash_online" if online else "gqa_flash",
    )(q, k, v)


def _proj_res_kernel(h_ref, x_ref, w_ref, ga_ref, o_ref):
    o_ref[...] = h_ref[...] + ga_ref[...] * _dot(x_ref[...], w_ref[...])


def _proj_res(h, x, w, ga):
    b, l, d = h.shape
    tm = _row_tile(l, 512)
    row = lambda w_: pl.BlockSpec((None, tm, w_), lambda bi, i: (bi, i, 0))
    return pl.pallas_call(
        _proj_res_kernel,
        grid=(b, l // tm),
        in_specs=[row(d), row(x.shape[-1]), _const_spec(w.shape),
                  pl.BlockSpec((None, 1, d), lambda bi, i: (bi, 0, 0))],
        out_specs=row(d),
        out_shape=jax.ShapeDtypeStruct((b, l, d), F32),
        compiler_params=_params("parallel", "parallel"),
        name="att_out",
    )(h, x, w, ga)


def _rope_tables(n_tokens):
    rows = n_tokens // GRID_W
    half = HEAD_DIM // 2
    r, c = jnp.meshgrid(jnp.arange(rows), jnp.arange(GRID_W), indexing='ij')
    inv = ROPE_THETA ** (-jnp.arange(0, half, 2, dtype=F32) / half)
    ang = jnp.concatenate([r.reshape(-1, 1).astype(F32) * inv,
                           c.reshape(-1, 1).astype(F32) * inv], axis=-1)
    cos, sin = jnp.cos(ang), jnp.sin(ang)
    return jnp.concatenate([cos, cos], axis=-1), jnp.concatenate([-sin, sin], axis=-1)


def kernel(x, c, ctx, c_ctx, w_mod, b_mod, g_norm_mix, g_norm_ffn, g_norm_final, w_even_in, w_gla_gate,
           b_gla_gate, g_gla_out, w_even_out, w_qkv, g_q, g_k, w_att_out, w_ffn_up, w_ffn_conv, b_ffn_conv,
           w_ffn_down):
    bsz, seq, d = x.shape
    assert w_mod.shape[0] == 2, "two layers: one even (Fourier || GLA) and one odd (attention)"
    fw = FOURIER_GROUPS * LANE
    kw = GLA_HEADS * GLA_DK
    vw = GLA_HEADS * GLA_DV
    main_w = fw + 2 * kw + 2 * vw

    rows = -(-(bsz + 1) // 8) * 8
    cv = jnp.zeros((rows, d), F32).at[:bsz].set(c).at[bsz].set(c_ctx)
    mod = _modulation(cv, w_mod, b_mod).reshape(2, rows, N_MOD, d)
    lat = lambda i, j: mod[i, :bsz, j][:, None, :]
    cx = lambda i, j: mod[i, bsz:bsz + 1, j][:, None, :]
    row2 = lambda v: v.reshape(1, -1)

    w_in = w_even_in[0]
    wm = w_in[:, :main_w].astype(BF16)
    wz = jnp.zeros((d, LANE), F32).at[:, :GLA_GATE_RANK].set(w_in[:, main_w:]).astype(BF16)
    wg = jnp.zeros((LANE, 2 * kw), F32).at[:GLA_GATE_RANK].set(
        jnp.concatenate([w_gla_gate[0, 0], w_gla_gate[0, 1]], axis=-1)).astype(BF16)
    bg = b_gla_gate[0].reshape(1, 2 * kw)
    nch = np.arange(LANE)
    angc = 2.0 * np.pi * ((nch[:, None] * nch[None, :]) % LANE) / LANE
    cs = _table(np.concatenate([np.cos(angc), np.sin(angc)], axis=1) / math.sqrt(LANE))
    ic = np.arange(GLA_CHUNK)
    tril = jnp.asarray(ic[:, None] >= ic[None, :], BF16)
    triu = jnp.asarray(ic[:, None] <= ic[None, :], BF16)
    w_out = w_even_out[0].astype(BF16)
    gg = row2(g_gla_out[0])
    ffn_w = [(w_ffn_up[i].astype(BF16), w_ffn_conv[i], row2(b_ffn_conv[i]), w_ffn_down[i].astype(BF16))
             for i in range(2)]

    def even_layer(h, sc1, sh1, ga1, sc2, sh2, ga2, s0f, s0b):
        af, bf, q, k, v, r, cf, cb = _inproj(h, sc1, sh1, row2(g_norm_mix[0]), wm, wz, cs, wg, bg, tril, triu)
        four = _position_dft(af, bf)
        of, ob, sf, sb = _gla(q, k, v, cf, cb, s0f, s0b)
        h = _mixout(h, four, of, ob, r, gg, w_out, ga1)
        h = _conv_ffn(h, sc2, sh2, ga2, row2(g_norm_ffn[0]), *ffn_w[0])
        return h, sf, sb

    zero_state = jnp.zeros((bsz, vw, kw), F32)
    h_ctx, s_f, s_b = even_layer(ctx, cx(0, 1), cx(0, 0), cx(0, 2), cx(0, 4), cx(0, 3), cx(0, 5),
                                 zero_state, zero_state)
    h_lat, _, _ = even_layer(x, lat(0, 1), lat(0, 0), lat(0, 2), lat(0, 4), lat(0, 3), lat(0, 5), s_f, s_b)

    wq = w_qkv[0].astype(BF16)
    cosf, sinf = _rope_tables(seq)
    gq, gk, gm = row2(g_q[0]), row2(g_k[0]), row2(g_norm_mix[1])
    k_c, v_c = _qkv(h_ctx, cx(1, 1), cx(1, 0), gm, wq, gq, gk, want_q=False)
    q_l, k_l, v_l = _qkv(h_lat, lat(1, 1), lat(1, 0), gm, wq, gq, gk, cosf, sinf)
    k_all = jnp.concatenate([k_c, k_l], axis=2)
    v_all = jnp.concatenate([v_c, v_l], axis=2)
    score_bound = HEAD_DIM * Q_SCALE * jnp.max(jnp.abs(g_q[0])) * jnp.max(jnp.abs(g_k[0]))
    att = lax.cond(score_bound <= ATT_PLAIN_MAX_LOG2,
                   lambda: _attention(q_l, k_all, v_all, online=False),
                   lambda: _attention(q_l, k_all, v_all, online=True))
    h_lat = _proj_res(h_lat, att, w_att_out[0].astype(BF16), lat(1, 2))
    return _conv_ffn(h_lat, lat(1, 4), lat(1, 3), lat(1, 5), row2(g_norm_ffn[1]), *ffn_w[1],
                     gfin=row2(g_norm_final))
```

```python
import functools
import math

import jax
import jax.numpy as jnp
import numpy as np
from jax import lax
from jax.experimental import pallas as pl
from jax.experimental.pallas import tpu as pltpu

F32 = jnp.float32
BF16 = jnp.bfloat16

EPS = 1e-6
N_MOD = 6
LANE = 128
SUBLANE_BF16 = 16
VMEM_LIMIT = 56 * 1024 * 1024

FOURIER_GROUPS = 4
GLA_HEADS = 4
GLA_DK = 64
GLA_DV = 128
GLA_GATE_RANK = 16
GLA_GATE_TEMP = 16.0
GLA_CHUNK = 64
HEAD_DIM = 128
ATT_KV_HEADS = 2
GRID_W = 64
ROPE_THETA = 10000.0
CONV_W = 3
Q_SCALE = HEAD_DIM ** -0.5 * math.log2(math.e)
ATT_PLAIN_MAX_LOG2 = 64.0

PROJ_SUB_ROWS = 256

_NT = (((1,), (1,)), ((), ()))
_TN = (((0,), (0,)), ((), ()))


def _dot(a, b):
    return jnp.dot(a, b, preferred_element_type=F32)


def _params(*sem):
    return pltpu.CompilerParams(dimension_semantics=sem, vmem_limit_bytes=VMEM_LIMIT)


def _const_spec(shape):
    nd = len(shape)
    return pl.BlockSpec(shape, lambda *_: (0,) * nd, pipeline_mode=pl.Buffered(1))


def _table(values):
    return jnp.asarray(values, F32).astype(BF16)


def _row_tile(n, cap):
    t = min(n, cap)
    assert n % t == 0, (n, t)
    return t


def _norm_mod(x, g, sc, sh):
    y = x * lax.rsqrt(jnp.mean(x * x, axis=-1, keepdims=True) + EPS)
    return y * (g * (1.0 + sc)) + sh


def _silu(x):
    return x * jax.nn.sigmoid(x)


def _mod_kernel(cv_ref, w_ref, b_ref, o_ref):
    s = _silu(cv_ref[...]).astype(BF16)
    o_ref[...] = _dot(s, w_ref[...].astype(BF16)) + b_ref[...]


def _modulation(cv, w_mod, b_mod):
    depth, d, n = w_mod.shape
    rows = cv.shape[0]
    tn = 1536
    assert n % tn == 0
    return pl.pallas_call(
        _mod_kernel,
        grid=(depth, n // tn),
        in_specs=[
            pl.BlockSpec((rows, d), lambda i, j: (0, 0)),
            pl.BlockSpec((None, d, tn), lambda i, j: (i, 0, j)),
            pl.BlockSpec((None, 1, tn), lambda i, j: (i, 0, j)),
        ],
        out_specs=pl.BlockSpec((None, rows, tn), lambda i, j: (i, 0, j)),
        out_shape=jax.ShapeDtypeStruct((depth, rows, n), F32),
        compiler_params=_params("parallel", "parallel"),
        name="adaln_mod",
    )(cv, w_mod, b_mod.reshape(depth, 1, n))


def _split3(x):
    hi = x.astype(BF16)
    r1 = x - hi.astype(F32)
    mid = r1.astype(BF16)
    lo = (r1 - mid.astype(F32)).astype(BF16)
    return hi, mid, lo


def _inproj_kernel(h_ref, sc_ref, sh_ref, g_ref, wm_ref, wz_ref, cs_ref, wg_ref, bg_ref,
                   tril_ref, triu_ref,
                   af_ref, bf_ref, q_ref, k_ref, v_ref, r_ref, cf_ref, cb_ref, dmin_ref):
    fw = FOURIER_GROUPS * LANE
    kw = GLA_HEADS * GLA_DK
    vw = GLA_HEADS * GLA_DV
    tm = h_ref.shape[0]
    sub = min(tm, PROJ_SUB_ROWS)
    tril = tril_ref[...]
    triu = triu_ref[...]

    def project(s):
        rows = slice(s * sub, (s + 1) * sub)
        a = _norm_mod(h_ref[rows, :], g_ref[...], sc_ref[...], sh_ref[...]).astype(BF16)
        return _dot(a, wm_ref[...]), _dot(a, wz_ref[...])

    def finish(s, p, z, dmin):
        rows = slice(s * sub, (s + 1) * sub)
        for g in range(FOURIER_GROUPS):
            ab = _dot(p[:, g * LANE:(g + 1) * LANE].astype(BF16), cs_ref[...])
            af_ref[rows, g * LANE:(g + 1) * LANE] = ab[:, :LANE].astype(BF16)
            bf_ref[rows, g * LANE:(g + 1) * LANE] = ab[:, LANE:].astype(BF16)
        q_ref[rows, :] = p[:, fw:fw + kw] * (GLA_DK ** -0.5)
        k_ref[rows, :] = p[:, fw + kw:fw + 2 * kw]
        v_ref[rows, :] = p[:, fw + 2 * kw:fw + 2 * kw + vw].astype(BF16)
        r_ref[rows, :] = p[:, fw + 2 * kw + vw:fw + 2 * kw + 2 * vw].astype(r_ref.dtype)
        zz = _dot(z.astype(BF16), wg_ref[...]) + bg_ref[...]
        loga = (jnp.minimum(zz, 0.0) - jnp.log1p(jnp.exp(-jnp.abs(zz)))) * (1.0 / GLA_GATE_TEMP)
        accf = None
        accb = None
        for term in _split3(loga):
            tf = _dot(tril, term[:, :kw])
            tb = _dot(triu, term[:, kw:])
            accf = tf if accf is None else accf + tf
            accb = tb if accb is None else accb + tb
        cf_ref[rows, :] = accf
        cb_ref[rows, :] = accb
        for c in range(sub // GLA_CHUNK):
            tot = jnp.minimum(accf[(c + 1) * GLA_CHUNK - 1:(c + 1) * GLA_CHUNK],
                              accb[c * GLA_CHUNK:c * GLA_CHUNK + 1])
            dmin = tot if dmin is None else jnp.minimum(dmin, tot)
        return dmin

    dmin = None
    pz = project(0)
    for s in range(tm // sub):
        pz_next = project(s + 1) if (s + 1) * sub < tm else None
        dmin = finish(s, *pz, dmin)
        pz = pz_next
    dmin_ref[...] = jnp.broadcast_to(dmin, dmin_ref.shape)


def _inproj(h, sc, sh, g, wm, wz, cs, wg, bg):
    b, l, d = h.shape
    tm = _row_tile(l, 512)
    ic = np.arange(min(tm, PROJ_SUB_ROWS))
    same_chunk = (ic[:, None] // GLA_CHUNK) == (ic[None, :] // GLA_CHUNK)
    tril = jnp.asarray(same_chunk & (ic[:, None] >= ic[None, :]), BF16)
    triu = jnp.asarray(same_chunk & (ic[:, None] <= ic[None, :]), BF16)
    per_batch = sc.shape[0] == b and b > 1
    mod_idx = (lambda bi, i: (bi, 0, 0)) if per_batch else (lambda bi, i: (0, 0, 0))
    row = lambda w: pl.BlockSpec((None, tm, w), lambda bi, i: (bi, i, 0))
    outs = [(512, BF16), (512, BF16), (256, F32), (256, F32), (512, BF16), (512, BF16), (256, F32), (256, F32)]
    kw = GLA_HEADS * GLA_DK
    return pl.pallas_call(
        _inproj_kernel,
        grid=(b, l // tm),
        in_specs=[
            row(d),
            pl.BlockSpec((None, 1, d), mod_idx),
            pl.BlockSpec((None, 1, d), mod_idx),
            _const_spec(g.shape), _const_spec(wm.shape), _const_spec(wz.shape), _const_spec(cs.shape),
            _const_spec(wg.shape), _const_spec(bg.shape), _const_spec(tril.shape), _const_spec(triu.shape),
        ],
        out_specs=[row(w) for w, _ in outs] + [pl.BlockSpec((None, None, 8, kw), lambda bi, i: (bi, i, 0, 0))],
        out_shape=[jax.ShapeDtypeStruct((b, l, w), dt) for w, dt in outs]
        + [jax.ShapeDtypeStruct((b, l // tm, 8, kw), F32)],
        compiler_params=_params("parallel", "parallel"),
        name="even_inproj",
    )(h, sc, sh, g, wm, wz, cs, wg, bg, tril, triu)


def _dft_dense_kernel(a_ref, b_ref, c_ref, s_ref, o_ref):
    o_ref[...] = (_dot(c_ref[...], a_ref[...]) - _dot(s_ref[...], b_ref[...])).astype(o_ref.dtype)


def _dft_dense(af, bf):
    b, l, w = af.shape
    n = np.arange(l)
    ang = 2.0 * np.pi * ((n[:, None] * n[None, :]) % l) / l
    c = _table(np.cos(ang) / math.sqrt(l))
    s = _table(np.sin(ang) / math.sqrt(l))
    blk = pl.BlockSpec((None, l, w), lambda bi: (bi, 0, 0))
    return pl.pallas_call(
        _dft_dense_kernel,
        grid=(b,),
        in_specs=[blk, blk, _const_spec(c.shape), _const_spec(s.shape)],
        out_specs=blk,
        out_shape=jax.ShapeDtypeStruct((b, l, w), BF16),
        compiler_params=_params("parallel"),
        name="dft_dense",
    )(af, bf, c, s)


DFT_N2 = LANE
DFT_NB = 8
DFT_KB = 8


def _dft_stage1_kernel(a_ref, b_ref, ma_ref, mb_ref, tc_ref, ts_ref, y_ref):
    n1 = a_ref.shape[0]
    y = _dot(ma_ref[...], a_ref[...]) + _dot(mb_ref[...], b_ref[...])
    for j in range(DFT_NB):
        tc = tc_ref[:, j * LANE:(j + 1) * LANE]
        ts = ts_ref[:, j * LANE:(j + 1) * LANE]
        for g in range(FOURIER_GROUPS):
            sl = slice((j * FOURIER_GROUPS + g) * LANE, (j * FOURIER_GROUPS + g + 1) * LANE)
            yr = y[:n1, sl]
            yi = y[n1:, sl]
            y_ref[:n1, sl] = (yr * tc + yi * ts).astype(BF16)
            y_ref[n1:, sl] = (yi * tc - yr * ts).astype(BF16)


def _dft_stage2_kernel(yr_ref, yi_ref, c_ref, s_ref, o_ref):
    w = yr_ref.shape[-1]
    for kk in range(DFT_KB):
        o = _dot(c_ref[...], yr_ref[kk]) + _dot(s_ref[...], yi_ref[kk])
        o_ref[:, kk * w:(kk + 1) * w] = o.astype(o_ref.dtype)


def _dft_factored(af, bf):
    b, l, w = af.shape
    n1 = l // DFT_N2
    assert l % DFT_N2 == 0 and n1 % SUBLANE_BF16 == 0 and n1 % DFT_KB == 0
    k = np.arange(n1)
    ang1 = 2.0 * np.pi * ((k[:, None] * k[None, :]) % n1) / n1
    c1, s1 = np.cos(ang1), np.sin(ang1)
    ma = _table(np.concatenate([c1, -s1], axis=0))
    mb = _table(np.concatenate([-s1, -c1], axis=0))
    n2 = np.arange(DFT_N2)
    angt = 2.0 * np.pi * (k[:, None] * n2[None, :]) / l
    scale = 1.0 / math.sqrt(l)
    tc = jnp.asarray(np.repeat(np.cos(angt) * scale, LANE, axis=1), F32)
    ts = jnp.asarray(np.repeat(np.sin(angt) * scale, LANE, axis=1), F32)
    ang2 = 2.0 * np.pi * ((n2[:, None] * n2[None, :]) % DFT_N2) / DFT_N2
    c2 = _table(np.cos(ang2))
    s2 = _table(np.sin(ang2))

    a3 = af.reshape(b, n1, DFT_N2 * w)
    b3 = bf.reshape(b, n1, DFT_N2 * w)
    in_blk = pl.BlockSpec((None, n1, DFT_NB * w), lambda bi, j: (bi, 0, j))
    y = pl.pallas_call(
        _dft_stage1_kernel,
        grid=(b, DFT_N2 // DFT_NB),
        in_specs=[
            in_blk, in_blk, _const_spec(ma.shape), _const_spec(mb.shape),
            pl.BlockSpec((n1, DFT_NB * LANE), lambda bi, j: (0, j)),
            pl.BlockSpec((n1, DFT_NB * LANE), lambda bi, j: (0, j)),
        ],
        out_specs=pl.BlockSpec((None, 2 * n1, DFT_NB * w), lambda bi, j: (bi, 0, j)),
        out_shape=jax.ShapeDtypeStruct((b, 2 * n1, DFT_N2 * w), BF16),
        compiler_params=_params("parallel", "parallel"),
        name="dft_stage1",
    )(a3, b3, ma, mb, tc, ts)

    y4 = y.reshape(b, 2 * n1, DFT_N2, w)
    nkb = n1 // DFT_KB
    out = pl.pallas_call(
        _dft_stage2_kernel,
        grid=(b, nkb),
        in_specs=[
            pl.BlockSpec((None, DFT_KB, DFT_N2, w), lambda bi, i: (bi, i, 0, 0)),
            pl.BlockSpec((None, DFT_KB, DFT_N2, w), lambda bi, i: (bi, nkb + i, 0, 0)),
            _const_spec(c2.shape), _const_spec(s2.shape),
        ],
        out_specs=pl.BlockSpec((None, DFT_N2, DFT_KB * w), lambda bi, i: (bi, 0, i)),
        out_shape=jax.ShapeDtypeStruct((b, DFT_N2, n1 * w), BF16),
        compiler_params=_params("parallel", "parallel"),
        name="dft_stage2",
    )(y4, y4, c2, s2)
    return out.reshape(b, l, w)


def _position_dft(af, bf):
    l = af.shape[1]
    if l % (DFT_N2 * SUBLANE_BF16) == 0:
        return _dft_factored(af, bf)
    return _dft_dense(af, bf)


GLA_FACTOR_MAX_DECAY = 80.0


def _gla_chunk(q_ref, k_ref, v_ref, c_ref, r0, s_ref, head_masks, tri_mask, state_mask, pair_sum, forward,
               factored):
    c = GLA_CHUNK
    rows = pl.ds(r0, c)
    q, k, v, cum = q_ref[rows, :], k_ref[rows, :], v_ref[rows, :], c_ref[rows, :]
    tot = cum[c - 1:c] if forward else cum[0:1]
    qe = q * jnp.exp(cum)
    kd = (k * jnp.exp(tot - cum)).astype(BF16)
    if factored:
        ke = (k * jnp.exp(-cum)).astype(BF16)
        q_stack = jnp.concatenate([qe * hm for hm in head_masks], axis=0).astype(BF16)
        att = lax.dot_general(q_stack, ke, _NT, preferred_element_type=F32)
        att = jnp.where(tri_mask, att, 0.0).astype(BF16)
        o_full = _dot(att, v)
        o_intra = jnp.concatenate(
            [o_full[h * c:(h + 1) * c, h * GLA_DV:(h + 1) * GLA_DV] for h in range(GLA_HEADS)], axis=1)
    else:
        ri = lax.broadcasted_iota(jnp.int32, (c, 1), 0)
        v32 = v.astype(F32)

        def row_of(x, j):
            return jnp.sum(jnp.where(ri == j, x, 0.0), axis=0, keepdims=True)

        def key_row(j, o):
            kj, cj, vj = row_of(k, j), row_of(cum, j), row_of(v32, j)
            live = (ri >= j) if forward else (ri <= j)
            e = q * kj * jnp.exp(jnp.where(live, cum - cj, -jnp.inf))
            return o + _dot(e.astype(BF16), pair_sum) * vj

        o_intra = lax.fori_loop(0, c, key_row, jnp.zeros((c, v.shape[1]), F32))
    s = s_ref[...]
    o_inter = lax.dot_general(qe.astype(BF16), s.astype(BF16), _NT, preferred_element_type=F32)
    upd = lax.dot_general(v, kd, _TN, preferred_element_type=F32)
    s_ref[...] = s * jnp.exp(tot) + jnp.where(state_mask, upd, 0.0)
    return o_intra + o_inter


def _gla_kernel(qf_ref, kf_ref, vf_ref, cf_ref, qb_ref, kb_ref, vb_ref, cb_ref, s0f_ref, s0b_ref,
                of_ref, ob_ref, sf_ref, sb_ref, *, factored):
    t = pl.program_id(1)
    c = GLA_CHUNK
    kw = GLA_HEADS * GLA_DK
    vw = GLA_HEADS * GLA_DV

    @pl.when(t == 0)
    def _():
        sf_ref[...] = s0f_ref[...]
        sb_ref[...] = s0b_ref[...]

    lane = lax.broadcasted_iota(jnp.int32, (1, kw), 1)
    head_masks = [(lane // GLA_DK == h).astype(F32) for h in range(GLA_HEADS)]
    ri = lax.broadcasted_iota(jnp.int32, (GLA_HEADS * c, c), 0) % c
    ci = lax.broadcasted_iota(jnp.int32, (GLA_HEADS * c, c), 1)
    tril = ci <= ri
    triu = ci >= ri
    sr = lax.broadcasted_iota(jnp.int32, (vw, kw), 0) // GLA_DV
    scol = lax.broadcasted_iota(jnp.int32, (vw, kw), 1) // GLA_DK
    state_mask = sr == scol
    pr = lax.broadcasted_iota(jnp.int32, (kw, vw), 0) // GLA_DK
    pc = lax.broadcasted_iota(jnp.int32, (kw, vw), 1) // GLA_DV
    pair_sum = (pr == pc).astype(BF16)

    nc = qf_ref.shape[0] // c
    for i in range(nc):
        of_ref[pl.ds(i * c, c), :] = _gla_chunk(
            qf_ref, kf_ref, vf_ref, cf_ref, i * c, sf_ref, head_masks, tril, state_mask, pair_sum, True, factored
        ).astype(of_ref.dtype)
        rb = (nc - 1 - i) * c
        ob_ref[pl.ds(rb, c), :] = _gla_chunk(
            qb_ref, kb_ref, vb_ref, cb_ref, rb, sb_ref, head_masks, triu, state_mask, pair_sum, False, factored
        ).astype(ob_ref.dtype)


def _gla_call(q, k, v, cf, cb, s0f, s0b, factored):
    b, l, kw = q.shape
    vw = v.shape[-1]
    tl = _row_tile(l, 512)
    nt = l // tl
    fwd = lambda w: pl.BlockSpec((None, tl, w), lambda bi, t: (bi, t, 0))
    bwd = lambda w: pl.BlockSpec((None, tl, w), lambda bi, t: (bi, nt - 1 - t, 0))
    st = pl.BlockSpec((None, vw, kw), lambda bi, t: (bi, 0, 0))
    return pl.pallas_call(
        functools.partial(_gla_kernel, factored=factored),
        grid=(b, nt),
        in_specs=[fwd(kw), fwd(kw), fwd(vw), fwd(kw), bwd(kw), bwd(kw), bwd(vw), bwd(kw), st, st],
        out_specs=[fwd(vw), bwd(vw), st, st],
        out_shape=[jax.ShapeDtypeStruct((b, l, vw), BF16), jax.ShapeDtypeStruct((b, l, vw), BF16),
                   jax.ShapeDtypeStruct((b, vw, kw), F32), jax.ShapeDtypeStruct((b, vw, kw), F32)],
        compiler_params=_params("parallel", "arbitrary"),
        name="gla_scan" if factored else "gla_scan_pairwise",
    )(q, k, v, cf, q, k, v, cb, s0f, s0b)


def _gla(q, k, v, cf, cb, dmin, s0f, s0b):
    args = (q, k, v, cf, cb, s0f, s0b)
    return lax.cond(jnp.min(dmin) >= -GLA_FACTOR_MAX_DECAY,
                    lambda: _gla_call(*args, factored=True),
                    lambda: _gla_call(*args, factored=False))


def _mixout_kernel(h_ref, f_ref, of_ref, ob_ref, r_ref, gg_ref, w_ref, ga_ref, o_ref):
    fw = f_ref.shape[-1]
    o = of_ref[...].astype(F32) + ob_ref[...].astype(F32)
    parts = []
    for hd in range(GLA_HEADS):
        oh = o[:, hd * GLA_DV:(hd + 1) * GLA_DV]
        parts.append(oh * lax.rsqrt(jnp.mean(oh * oh, axis=-1, keepdims=True) + EPS))
    on = jnp.concatenate(parts, axis=1) * gg_ref[...]
    on = (on * _silu(r_ref[...].astype(F32))).astype(BF16)
    y = _dot(f_ref[...], w_ref[:fw, :]) + _dot(on, w_ref[fw:, :])
    o_ref[...] = h_ref[...] + ga_ref[...] * y


def _mixout(h, four, of, ob, r, gg, w, ga):
    b, l, d = h.shape
    tm = _row_tile(l, 512)
    per_batch = ga.shape[0] == b and b > 1
    mod_idx = (lambda bi, i: (bi, 0, 0)) if per_batch else (lambda bi, i: (0, 0, 0))
    row = lambda w_: pl.BlockSpec((None, tm, w_), lambda bi, i: (bi, i, 0))
    return pl.pallas_call(
        _mixout_kernel,
        grid=(b, l // tm),
        in_specs=[row(d), row(four.shape[-1]), row(of.shape[-1]), row(ob.shape[-1]), row(r.shape[-1]),
                  _const_spec(gg.shape), _const_spec(w.shape), pl.BlockSpec((None, 1, d), mod_idx)],
        out_specs=row(d),
        out_shape=jax.ShapeDtypeStruct((b, l, d), F32),
        compiler_params=_params("parallel", "parallel"),
        name="even_mixout",
    )(h, four, of, ob, r, gg, w, ga)


FFN_HALO = SUBLANE_BF16
FFN_TF = 256
FFN_PSLOTS = 4
FFN_DOWN_CHUNKS = 2


def _ffn_kernel(h_ref, hp_ref, hn_ref, sc_ref, sh_ref, ga_ref, g_ref, wup_ref, wc_ref, bc_ref, wdn_ref,
                *rest, final):
    if final:
        gfin_ref, o_ref, fext_ref, act_ref, *p_ref = rest
    else:
        o_ref, fext_ref, act_ref, *p_ref = rest
    i = pl.program_id(1)
    last = pl.num_programs(1) - 1
    tm = h_ref.shape[0]
    dff = wdn_ref.shape[0]
    hal = FFN_HALO
    g, sc, sh = g_ref[...], sc_ref[...], sh_ref[...]
    h = h_ref[...]
    fp = _norm_mod(hp_ref[...], g, sc, sh) * (i > 0).astype(F32)
    fn = _norm_mod(hn_ref[...], g, sc, sh) * (i < last).astype(F32)
    fext_ref[0:hal, :] = fp.astype(BF16)
    fext_ref[hal:hal + tm, :] = _norm_mod(h, g, sc, sh).astype(BF16)
    fext_ref[hal + tm:, :] = fn.astype(BF16)
    acc = None
    nch = dff // FFN_TF
    flushed = 0

    def cols_of(c, half):
        return slice(half * dff + c * FFN_TF, half * dff + (c + 1) * FFN_TF)

    def up(c):
        for half in range(2):
            p_ref[(2 * c + half) % FFN_PSLOTS][...] = _dot(fext_ref[...], wup_ref[:, cols_of(c, half)])

    def conv(c, half):
        pr = p_ref[(2 * c + half) % FFN_PSLOTS]
        cs = cols_of(c, half)
        return (pr[pl.ds(hal - 1, tm), :] * wc_ref[0:1, cs]
                + pr[pl.ds(hal, tm), :] * wc_ref[1:2, cs]
                + pr[pl.ds(hal + 1, tm), :] * wc_ref[2:3, cs]
                + bc_ref[:, cs])

    def down(done):
        nonlocal acc, flushed
        ks = slice(flushed * FFN_TF, done * FFN_TF)
        d = _dot(act_ref[:, ks], wdn_ref[ks, :])
        acc = d if acc is None else acc + d
        flushed = done

    up(0)
    for c in range(nch):
        if c + 1 < nch:
            up(c + 1)
        if c - flushed >= FFN_DOWN_CHUNKS:
            down(c)
        act_ref[:, c * FFN_TF:(c + 1) * FFN_TF] = (_silu(conv(c, 0)) * conv(c, 1)).astype(BF16)
    down(nch)
    out = h + ga_ref[...] * acc
    if final:
        out = out * lax.rsqrt(jnp.mean(out * out, axis=-1, keepdims=True) + EPS) * gfin_ref[...]
    o_ref[...] = out


def _conv_ffn(h, sc, sh, ga, g, wup, wc, bc, wdn, gfin=None):
    b, l, d = h.shape
    tm = _row_tile(l, 512)
    assert tm % FFN_HALO == 0 and wdn.shape[0] % FFN_TF == 0 and CONV_W == 3
    per_batch = ga.shape[0] == b and b > 1
    mod_idx = (lambda bi, i: (bi, 0, 0)) if per_batch else (lambda bi, i: (0, 0, 0))
    hb = tm // FFN_HALO
    nh = l // FFN_HALO
    mod = pl.BlockSpec((None, 1, d), mod_idx)
    in_specs = [
        pl.BlockSpec((None, tm, d), lambda bi, i: (bi, i, 0)),
        pl.BlockSpec((None, FFN_HALO, d), lambda bi, i: (bi, jnp.maximum(i * hb - 1, 0), 0)),
        pl.BlockSpec((None, FFN_HALO, d), lambda bi, i: (bi, jnp.minimum((i + 1) * hb, nh - 1), 0)),
        mod, mod, mod,
        _const_spec(g.shape), _const_spec(wup.shape), _const_spec(wc.shape), _const_spec(bc.shape),
        _const_spec(wdn.shape),
    ]
    args = [h, h, h, sc, sh, ga, g, wup, wc, bc, wdn]
    if gfin is not None:
        in_specs.append(_const_spec(gfin.shape))
        args.append(gfin)
    return pl.pallas_call(
        functools.partial(_ffn_kernel, final=gfin is not None),
        grid=(b, l // tm),
        in_specs=in_specs,
        out_specs=pl.BlockSpec((None, tm, d), lambda bi, i: (bi, i, 0)),
        out_shape=jax.ShapeDtypeStruct((b, l, d), F32),
        scratch_shapes=([pltpu.VMEM((tm + 2 * FFN_HALO, d), BF16), pltpu.VMEM((tm, wdn.shape[0]), BF16)]
                        + [pltpu.VMEM((tm + 2 * FFN_HALO, FFN_TF), F32)] * FFN_PSLOTS),
        compiler_params=_params("parallel", "parallel"),
        name="conv_ffn_final" if gfin is not None else "conv_ffn",
    )(*args)


def _head_norm(t, g):
    return t * lax.rsqrt(jnp.mean(t * t, axis=-1, keepdims=True) + EPS) * g


def _rope(t, cosf, sinf):
    return t * cosf + pltpu.roll(t, HEAD_DIM // 2, 1) * sinf


def _qkv_kernel(h_ref, sc_ref, sh_ref, g_ref, w_ref, gq_ref, gk_ref, *rest, n_q, rope):
    if rope:
        cos_ref, sin_ref = rest[:2]
        rest = rest[2:]
    if n_q:
        q_ref, k_ref, v_ref = rest
    else:
        k_ref, v_ref = rest
    col = lambda j: slice(j * HEAD_DIM, (j + 1) * HEAD_DIM)
    w_q = w_ref.shape[1] - 2 * ATT_KV_HEADS * HEAD_DIM
    c0 = 0 if n_q else w_q
    tm = h_ref.shape[0]
    sub = min(tm, PROJ_SUB_ROWS)

    def project(s):
        rows = slice(s * sub, (s + 1) * sub)
        a = _norm_mod(h_ref[rows, :], g_ref[...], sc_ref[...], sh_ref[...]).astype(BF16)
        return _dot(a, w_ref[:, c0:])

    def finish(s, p):
        rows = slice(s * sub, (s + 1) * sub)
        cosf = cos_ref[rows, :] if rope else None
        sinf = sin_ref[rows, :] if rope else None
        for hd in range(n_q):
            t = _head_norm(p[:, col(hd)], gq_ref[...])
            if rope:
                t = _rope(t, cosf, sinf)
            q_ref[hd, rows, :] = (t * Q_SCALE).astype(BF16)
        for hd in range(ATT_KV_HEADS):
            t = _head_norm(p[:, col(n_q + hd)], gk_ref[...])
            if rope:
                t = _rope(t, cosf, sinf)
            k_ref[hd, rows, :] = t.astype(BF16)
            v_ref[hd, rows, :] = p[:, col(n_q + ATT_KV_HEADS + hd)].astype(BF16)

    p = project(0)
    for s in range(tm // sub):
        p_next = project(s + 1) if (s + 1) * sub < tm else None
        finish(s, p)
        p = p_next


def _qkv(h, sc, sh, g, w, gq, gk, cosf=None, sinf=None, want_q=True):
    b, l, d = h.shape
    tm = _row_tile(l, 512)
    n_heads = (w.shape[1] - 2 * ATT_KV_HEADS * HEAD_DIM) // HEAD_DIM
    n_q = n_heads if want_q else 0
    rope = cosf is not None
    per_batch = sc.shape[0] == b and b > 1
    mod_idx = (lambda bi, i: (bi, 0, 0)) if per_batch else (lambda bi, i: (0, 0, 0))
    in_specs = [
        pl.BlockSpec((None, tm, d), lambda bi, i: (bi, i, 0)),
        pl.BlockSpec((None, 1, d), mod_idx), pl.BlockSpec((None, 1, d), mod_idx),
        _const_spec(g.shape), _const_spec(w.shape), _const_spec(gq.shape), _const_spec(gk.shape),
    ]
    args = [h, sc, sh, g, w, gq, gk]
    if rope:
        in_specs += [pl.BlockSpec((tm, HEAD_DIM), lambda bi, i: (i, 0))] * 2
        args += [cosf, sinf]
    heads = lambda n: pl.BlockSpec((None, n, tm, HEAD_DIM), lambda bi, i: (bi, 0, i, 0))
    shape = lambda n: jax.ShapeDtypeStruct((b, n, l, HEAD_DIM), BF16)
    out_specs = [heads(ATT_KV_HEADS), heads(ATT_KV_HEADS)]
    out_shape = [shape(ATT_KV_HEADS), shape(ATT_KV_HEADS)]
    if n_q:
        out_specs = [heads(n_q)] + out_specs
        out_shape = [shape(n_q)] + out_shape
    return pl.pallas_call(
        functools.partial(_qkv_kernel, n_q=n_q, rope=rope),
        grid=(b, l // tm),
        in_specs=in_specs,
        out_specs=out_specs,
        out_shape=out_shape,
        compiler_params=_params("parallel", "parallel"),
        name="qkv_rope" if rope else "kv_ctx",
    )(*args)


def _attn_kernel(q_ref, k_ref, v_ref, o_ref, *scratch, tk, online):
    grp, tq, hd = q_ref.shape
    nlt = tk // LANE
    nk = k_ref.shape[0] // tk
    assert grp % 2 == 0 and hd == LANE
    l_ref, acc_ref = scratch[:grp], scratch[grp:2 * grp]
    p_ref = scratch[2 * grp:2 * grp + 2]
    if online:
        rest = scratch[2 * grp + 2:]
        m_ref, al_ref, s_ref, mx_ref = rest[:grp], rest[grp:2 * grp], rest[2 * grp:2 * grp + 2], rest[2 * grp + 2:]
    for g in range(grp):
        l_ref[g][...] = jnp.zeros(l_ref[g].shape, F32)
        acc_ref[g][...] = jnp.zeros(acc_ref[g].shape, F32)
        if online:
            m_ref[g][...] = jnp.full(m_ref[g].shape, -jnp.inf, F32)

    def chunk(ref, j):
        if isinstance(j, int):
            return ref[j * tk:(j + 1) * tk, :]
        start = pl.multiple_of(jnp.clip(j, 0, nk - 1) * tk, tk)
        return ref[pl.ds(start, tk), :]

    def lane_tiles(x):
        return [x[:, t * LANE:(t + 1) * LANE] for t in range(nlt)]

    def scores(g, j, slot):
        s = lax.dot_general(q_ref[g], chunk(k_ref, j), _NT, preferred_element_type=F32)
        if online:
            s_ref[slot][...] = s
            mx_ref[slot][...] = functools.reduce(jnp.maximum, lane_tiles(s))
        else:
            p = jnp.exp2(s)
            p_ref[slot][...] = p.astype(BF16)
            l_ref[g][...] += functools.reduce(jnp.add, lane_tiles(p))

    def softmax(g, slot):
        m_old = m_ref[g][...]
        m_new = jnp.maximum(m_old, jnp.max(mx_ref[slot][...], axis=-1, keepdims=True))
        alpha = jnp.exp2(m_old - m_new)
        al_ref[g][...] = alpha
        m_ref[g][...] = m_new
        ps = None
        for t in range(nlt):
            p = jnp.exp2(s_ref[slot][:, t * LANE:(t + 1) * LANE] - m_new)
            p_ref[slot][:, t * LANE:(t + 1) * LANE] = p.astype(BF16)
            ps = p if ps is None else ps + p
        l_ref[g][...] = alpha * l_ref[g][...] + ps

    def values(g, j, slot):
        pv = _dot(p_ref[slot][...], chunk(v_ref, j))
        if online:
            acc_ref[g][...] = al_ref[g][...] * acc_ref[g][...] + pv
        else:
            acc_ref[g][...] += pv

    scores(0, 0, 0)
    if online:
        al_ref[grp - 1][...] = jnp.ones(al_ref[grp - 1].shape, F32)
        p_ref[(grp - 1) % 2][...] = jnp.zeros(p_ref[(grp - 1) % 2].shape, BF16)

        def body(j, carry):
            for g in range(grp):
                scores((g + 1) % grp, j + (g + 1) // grp, (g + 1) % 2)
                softmax(g, g % 2)
                values((g - 1) % grp, j + (g - 1) // grp, (g - 1) % 2)
            return carry

        lax.fori_loop(0, nk, body, 0)
        values(grp - 1, nk - 1, (grp - 1) % 2)
    else:
        for j in range(nk):
            for g in range(grp):
                if g + 1 < grp:
                    scores(g + 1, j, (g + 1) % 2)
                elif j + 1 < nk:
                    scores(0, j + 1, 0)
                values(g, j, g % 2)
    for g in range(grp):
        out = acc_ref[g][...] / jnp.sum(l_ref[g][...], axis=-1, keepdims=True)
        o_ref[:, g * hd:(g + 1) * hd] = out.astype(o_ref.dtype)


def _attention(q, k, v, online):
    b, h, s, hd = q.shape
    hkv, lk = k.shape[1], k.shape[2]
    grp = h // hkv
    tq = _row_tile(s, 256)
    tk = next(t for t in (768, 512, 256, 128) if lk % t == 0)
    kv = pl.BlockSpec((None, None, lk, hd), lambda bi, kh, i: (bi, kh, 0, 0))
    scratch = [pltpu.VMEM((tq, LANE), F32)] * (2 * grp) + [pltpu.VMEM((tq, tk), BF16)] * 2
    if online:
        scratch += ([pltpu.VMEM((tq, LANE), F32)] * (2 * grp) + [pltpu.VMEM((tq, tk), F32)] * 2
                    + [pltpu.VMEM((tq, LANE), F32)] * 2)
    return pl.pallas_call(
        functools.partial(_attn_kernel, tk=tk, online=online),
        grid=(b, hkv, s // tq),
        in_specs=[pl.BlockSpec((None, grp, tq, hd), lambda bi, kh, i: (bi, kh, i, 0)), kv, kv],
        out_specs=pl.BlockSpec((None, tq, grp * hd), lambda bi, kh, i: (bi, i, kh)),
        out_shape=jax.ShapeDtypeStruct((b, s, h * hd), BF16),
        scratch_shapes=scratch,
        compiler_params=_params("parallel", "parallel", "arbitrary"),
        name="gqa_flash_online" if online else "gqa_flash",
    )(q, k, v)


def _proj_res_kernel(h_ref, x_ref, w_ref, ga_ref, o_ref):
    o_ref[...] = h_ref[...] + ga_ref[...] * _dot(x_ref[...], w_ref[...])


def _proj_res(h, x, w, ga):
    b, l, d = h.shape
    tm = _row_tile(l, 512)
    row = lambda w_: pl.BlockSpec((None, tm, w_), lambda bi, i: (bi, i, 0))
    return pl.pallas_call(
        _proj_res_kernel,
        grid=(b, l // tm),
        in_specs=[row(d), row(x.shape[-1]), _const_spec(w.shape),
                  pl.BlockSpec((None, 1, d), lambda bi, i: (bi, 0, 0))],
        out_specs=row(d),
        out_shape=jax.ShapeDtypeStruct((b, l, d), F32),
        compiler_params=_params("parallel", "parallel"),
        name="att_out",
    )(h, x, w, ga)


def _rope_tables(n_tokens):
    rows = n_tokens // GRID_W
    half = HEAD_DIM // 2
    r, c = jnp.meshgrid(jnp.arange(rows), jnp.arange(GRID_W), indexing='ij')
    inv = ROPE_THETA ** (-jnp.arange(0, half, 2, dtype=F32) / half)
    ang = jnp.concatenate([r.reshape(-1, 1).astype(F32) * inv,
                           c.reshape(-1, 1).astype(F32) * inv], axis=-1)
    cos, sin = jnp.cos(ang), jnp.sin(ang)
    return jnp.concatenate([cos, cos], axis=-1), jnp.concatenate([-sin, sin], axis=-1)


def kernel(x, c, ctx, c_ctx, w_mod, b_mod, g_norm_mix, g_norm_ffn, g_norm_final, w_even_in, w_gla_gate,
           b_gla_gate, g_gla_out, w_even_out, w_qkv, g_q, g_k, w_att_out, w_ffn_up, w_ffn_conv, b_ffn_conv,
           w_ffn_down):
    bsz, seq, d = x.shape
    assert w_mod.shape[0] == 2, "two layers: one even (Fourier || GLA) and one odd (attention)"
    fw = FOURIER_GROUPS * LANE
    kw = GLA_HEADS * GLA_DK
    vw = GLA_HEADS * GLA_DV
    main_w = fw + 2 * kw + 2 * vw

    rows = -(-(bsz + 1) // 8) * 8
    cv = jnp.zeros((rows, d), F32).at[:bsz].set(c).at[bsz].set(c_ctx)
    mod = _modulation(cv, w_mod, b_mod).reshape(2, rows, N_MOD, d)
    lat = lambda i, j: mod[i, :bsz, j][:, None, :]
    cx = lambda i, j: mod[i, bsz:bsz + 1, j][:, None, :]
    row2 = lambda v: v.reshape(1, -1)

    w_in = w_even_in[0]
    wm = w_in[:, :main_w].astype(BF16)
    wz = jnp.zeros((d, LANE), F32).at[:, :GLA_GATE_RANK].set(w_in[:, main_w:]).astype(BF16)
    wg = jnp.zeros((LANE, 2 * kw), F32).at[:GLA_GATE_RANK].set(
        jnp.concatenate([w_gla_gate[0, 0], w_gla_gate[0, 1]], axis=-1)).astype(BF16)
    bg = b_gla_gate[0].reshape(1, 2 * kw)
    nch = np.arange(LANE)
    angc = 2.0 * np.pi * ((nch[:, None] * nch[None, :]) % LANE) / LANE
    cs = _table(np.concatenate([np.cos(angc), np.sin(angc)], axis=1) / math.sqrt(LANE))
    w_out = w_even_out[0].astype(BF16)
    gg = row2(g_gla_out[0])
    ffn_w = [(w_ffn_up[i].astype(BF16), w_ffn_conv[i], row2(b_ffn_conv[i]), w_ffn_down[i].astype(BF16))
             for i in range(2)]

    def even_layer(h, sc1, sh1, ga1, sc2, sh2, ga2, s0f, s0b):
        af, bf, q, k, v, r, cf, cb, dmin = _inproj(h, sc1, sh1, row2(g_norm_mix[0]), wm, wz, cs, wg, bg)
        four = _position_dft(af, bf)
        of, ob, sf, sb = _gla(q, k, v, cf, cb, dmin, s0f, s0b)
        h = _mixout(h, four, of, ob, r, gg, w_out, ga1)
        h = _conv_ffn(h, sc2, sh2, ga2, row2(g_norm_ffn[0]), *ffn_w[0])
        return h, sf, sb

    zero_state = jnp.zeros((bsz, vw, kw), F32)
    h_ctx, s_f, s_b = even_layer(ctx, cx(0, 1), cx(0, 0), cx(0, 2), cx(0, 4), cx(0, 3), cx(0, 5),
                                 zero_state, zero_state)
    h_lat, _, _ = even_layer(x, lat(0, 1), lat(0, 0), lat(0, 2), lat(0, 4), lat(0, 3), lat(0, 5), s_f, s_b)

    wq = w_qkv[0].astype(BF16)
    cosf, sinf = _rope_tables(seq)
    gq, gk, gm = row2(g_q[0]), row2(g_k[0]), row2(g_norm_mix[1])
    k_c, v_c = _qkv(h_ctx, cx(1, 1), cx(1, 0), gm, wq, gq, gk, want_q=False)
    q_l, k_l, v_l = _qkv(h_lat, lat(1, 1), lat(1, 0), gm, wq, gq, gk, cosf, sinf)
    k_all = jnp.concatenate([k_c, k_l], axis=2)
    v_all = jnp.concatenate([v_c, v_l], axis=2)
    score_bound = HEAD_DIM * Q_SCALE * jnp.max(jnp.abs(g_q[0])) * jnp.max(jnp.abs(g_k[0]))
    att = lax.cond(score_bound <= ATT_PLAIN_MAX_LOG2,
                   lambda: _attention(q_l, k_all, v_all, online=False),
                   lambda: _attention(q_l, k_all, v_all, online=True))
    h_lat = _proj_res(h_lat, att, w_att_out[0].astype(BF16), lat(1, 2))
    return _conv_ffn(h_lat, lat(1, 4), lat(1, 3), lat(1, 5), row2(g_norm_ffn[1]), *ffn_w[1],
                     gfin=row2(g_norm_final))
```

```python
import functools
import math

import jax
import jax.numpy as jnp
import numpy as np
from jax import lax
from jax.experimental import pallas as pl
from jax.experimental.pallas import tpu as pltpu

F32 = jnp.float32
BF16 = jnp.bfloat16

EPS = 1e-6
N_MOD = 6
LANE = 128
SUBLANE_BF16 = 16
VMEM_LIMIT = 56 * 1024 * 1024

FOURIER_GROUPS = 4
GLA_HEADS = 4
GLA_DK = 64
GLA_DV = 128
GLA_GATE_RANK = 16
GLA_GATE_TEMP = 16.0
GLA_CHUNK = 64
HEAD_DIM = 128
ATT_KV_HEADS = 2
GRID_W = 64
ROPE_THETA = 10000.0
CONV_W = 3
Q_SCALE = HEAD_DIM ** -0.5 * math.log2(math.e)
ATT_PLAIN_MAX_LOG2 = 64.0

RESIDUAL_TILE_ROWS = 1024
PROJ_TILE_ROWS = 512
PROJ_SUB_ROWS = 256

_NT = (((1,), (1,)), ((), ()))
_TN = (((0,), (0,)), ((), ()))


def _dot(a, b):
    return jnp.dot(a, b, preferred_element_type=F32)


def _params(*sem):
    return pltpu.CompilerParams(dimension_semantics=sem, vmem_limit_bytes=VMEM_LIMIT)


def _const_spec(shape):
    nd = len(shape)
    return pl.BlockSpec(shape, lambda *_: (0,) * nd, pipeline_mode=pl.Buffered(1))


def _table(values):
    return jnp.asarray(values, F32).astype(BF16)


def _row_tile(n, cap):
    t = min(n, cap)
    assert n % t == 0, (n, t)
    return t


def _norm_mod(x, g, sc, sh):
    y = x * lax.rsqrt(jnp.mean(x * x, axis=-1, keepdims=True) + EPS)
    return y * (g * (1.0 + sc)) + sh


def _silu(x):
    return x * jax.nn.sigmoid(x)


def _mod_kernel(cv_ref, w_ref, b_ref, o_ref):
    s = _silu(cv_ref[...]).astype(BF16)
    o_ref[...] = _dot(s, w_ref[...].astype(BF16)) + b_ref[...]


def _modulation(cv, w_mod, b_mod):
    depth, d, n = w_mod.shape
    rows = cv.shape[0]
    tn = 1536
    assert n % tn == 0
    return pl.pallas_call(
        _mod_kernel,
        grid=(depth, n // tn),
        in_specs=[
            pl.BlockSpec((rows, d), lambda i, j: (0, 0)),
            pl.BlockSpec((None, d, tn), lambda i, j: (i, 0, j)),
            pl.BlockSpec((None, 1, tn), lambda i, j: (i, 0, j)),
        ],
        out_specs=pl.BlockSpec((None, rows, tn), lambda i, j: (i, 0, j)),
        out_shape=jax.ShapeDtypeStruct((depth, rows, n), F32),
        compiler_params=_params("parallel", "parallel"),
        name="adaln_mod",
    )(cv, w_mod, b_mod.reshape(depth, 1, n))


def _split3(x):
    hi = x.astype(BF16)
    r1 = x - hi.astype(F32)
    mid = r1.astype(BF16)
    lo = (r1 - mid.astype(F32)).astype(BF16)
    return hi, mid, lo


def _inproj_kernel(h_ref, sc_ref, sh_ref, g_ref, wm_ref, wz_ref, cs_ref, wg_ref, bg_ref,
                   tril_ref, triu_ref,
                   af_ref, bf_ref, q_ref, k_ref, v_ref, r_ref, cf_ref, cb_ref, dmin_ref):
    fw = FOURIER_GROUPS * LANE
    kw = GLA_HEADS * GLA_DK
    vw = GLA_HEADS * GLA_DV
    tm = h_ref.shape[0]
    sub = min(tm, PROJ_SUB_ROWS)
    tril = tril_ref[...]
    triu = triu_ref[...]

    def project(s):
        rows = slice(s * sub, (s + 1) * sub)
        a = _norm_mod(h_ref[rows, :], g_ref[...], sc_ref[...], sh_ref[...]).astype(BF16)
        return _dot(a, wm_ref[...]), _dot(a, wz_ref[...])

    def finish(s, p, z, dmin):
        rows = slice(s * sub, (s + 1) * sub)
        for g in range(FOURIER_GROUPS):
            ab = _dot(p[:, g * LANE:(g + 1) * LANE].astype(BF16), cs_ref[...])
            af_ref[rows, g * LANE:(g + 1) * LANE] = ab[:, :LANE].astype(BF16)
            bf_ref[rows, g * LANE:(g + 1) * LANE] = ab[:, LANE:].astype(BF16)
        q_ref[rows, :] = p[:, fw:fw + kw] * (GLA_DK ** -0.5)
        k_ref[rows, :] = p[:, fw + kw:fw + 2 * kw]
        v_ref[rows, :] = p[:, fw + 2 * kw:fw + 2 * kw + vw].astype(BF16)
        r_ref[rows, :] = p[:, fw + 2 * kw + vw:fw + 2 * kw + 2 * vw].astype(r_ref.dtype)
        zz = _dot(z.astype(BF16), wg_ref[...]) + bg_ref[...]
        loga = (jnp.minimum(zz, 0.0) - jnp.log1p(jnp.exp(-jnp.abs(zz)))) * (1.0 / GLA_GATE_TEMP)
        accf = None
        accb = None
        for term in _split3(loga):
            tf = _dot(tril, term[:, :kw])
            tb = _dot(triu, term[:, kw:])
            accf = tf if accf is None else accf + tf
            accb = tb if accb is None else accb + tb
        cf_ref[rows, :] = accf
        cb_ref[rows, :] = accb
        for c in range(sub // GLA_CHUNK):
            tot = jnp.minimum(accf[(c + 1) * GLA_CHUNK - 1:(c + 1) * GLA_CHUNK],
                              accb[c * GLA_CHUNK:c * GLA_CHUNK + 1])
            dmin = tot if dmin is None else jnp.minimum(dmin, tot)
        return dmin

    dmin = None
    pz = project(0)
    for s in range(tm // sub):
        pz_next = project(s + 1) if (s + 1) * sub < tm else None
        dmin = finish(s, *pz, dmin)
        pz = pz_next
    dmin_ref[...] = jnp.broadcast_to(dmin, dmin_ref.shape)


def _inproj(h, sc, sh, g, wm, wz, cs, wg, bg):
    b, l, d = h.shape
    tm = _row_tile(l, PROJ_TILE_ROWS)
    ic = np.arange(min(tm, PROJ_SUB_ROWS))
    same_chunk = (ic[:, None] // GLA_CHUNK) == (ic[None, :] // GLA_CHUNK)
    tril = jnp.asarray(same_chunk & (ic[:, None] >= ic[None, :]), BF16)
    triu = jnp.asarray(same_chunk & (ic[:, None] <= ic[None, :]), BF16)
    per_batch = sc.shape[0] == b and b > 1
    mod_idx = (lambda bi, i: (bi, 0, 0)) if per_batch else (lambda bi, i: (0, 0, 0))
    row = lambda w: pl.BlockSpec((None, tm, w), lambda bi, i: (bi, i, 0))
    outs = [(512, BF16), (512, BF16), (256, F32), (256, F32), (512, BF16), (512, BF16), (256, F32), (256, F32)]
    kw = GLA_HEADS * GLA_DK
    return pl.pallas_call(
        _inproj_kernel,
        grid=(b, l // tm),
        in_specs=[
            row(d),
            pl.BlockSpec((None, 1, d), mod_idx),
            pl.BlockSpec((None, 1, d), mod_idx),
            _const_spec(g.shape), _const_spec(wm.shape), _const_spec(wz.shape), _const_spec(cs.shape),
            _const_spec(wg.shape), _const_spec(bg.shape), _const_spec(tril.shape), _const_spec(triu.shape),
        ],
        out_specs=[row(w) for w, _ in outs] + [pl.BlockSpec((None, None, 8, kw), lambda bi, i: (bi, i, 0, 0))],
        out_shape=[jax.ShapeDtypeStruct((b, l, w), dt) for w, dt in outs]
        + [jax.ShapeDtypeStruct((b, l // tm, 8, kw), F32)],
        compiler_params=_params("parallel", "parallel"),
        name="even_inproj",
    )(h, sc, sh, g, wm, wz, cs, wg, bg, tril, triu)


def _dft_dense_kernel(a_ref, b_ref, c_ref, s_ref, o_ref):
    o_ref[...] = (_dot(c_ref[...], a_ref[...]) - _dot(s_ref[...], b_ref[...])).astype(o_ref.dtype)


def _dft_dense(af, bf):
    b, l, w = af.shape
    n = np.arange(l)
    ang = 2.0 * np.pi * ((n[:, None] * n[None, :]) % l) / l
    c = _table(np.cos(ang) / math.sqrt(l))
    s = _table(np.sin(ang) / math.sqrt(l))
    blk = pl.BlockSpec((None, l, w), lambda bi: (bi, 0, 0))
    return pl.pallas_call(
        _dft_dense_kernel,
        grid=(b,),
        in_specs=[blk, blk, _const_spec(c.shape), _const_spec(s.shape)],
        out_specs=blk,
        out_shape=jax.ShapeDtypeStruct((b, l, w), BF16),
        compiler_params=_params("parallel"),
        name="dft_dense",
    )(af, bf, c, s)


DFT_N2 = LANE
DFT_NB = 16
DFT_KB = 8


def _dft_stage1_kernel(a_ref, b_ref, m_ref, tc_ref, ts_ref, y_ref):
    n1 = a_ref.shape[0]
    y = _dot(m_ref[...], jnp.concatenate([a_ref[...], b_ref[...]], axis=0))
    for j in range(DFT_NB):
        tc = tc_ref[:, j * LANE:(j + 1) * LANE]
        ts = ts_ref[:, j * LANE:(j + 1) * LANE]
        for g in range(FOURIER_GROUPS):
            sl = slice((j * FOURIER_GROUPS + g) * LANE, (j * FOURIER_GROUPS + g + 1) * LANE)
            yr = y[:n1, sl]
            yi = y[n1:, sl]
            y_ref[:n1, sl] = (yr * tc + yi * ts).astype(BF16)
            y_ref[n1:, sl] = (yi * tc - yr * ts).astype(BF16)


def _dft_stage2_kernel(yr_ref, yi_ref, c_ref, s_ref, o_ref):
    w = yr_ref.shape[-1]
    for kk in range(DFT_KB):
        o = _dot(c_ref[...], yr_ref[kk]) + _dot(s_ref[...], yi_ref[kk])
        o_ref[:, kk * w:(kk + 1) * w] = o.astype(o_ref.dtype)


def _dft_factored(af, bf):
    b, l, w = af.shape
    n1 = l // DFT_N2
    assert l % DFT_N2 == 0 and n1 % SUBLANE_BF16 == 0 and n1 % DFT_KB == 0
    k = np.arange(n1)
    ang1 = 2.0 * np.pi * ((k[:, None] * k[None, :]) % n1) / n1
    c1, s1 = np.cos(ang1), np.sin(ang1)
    m1 = _table(np.block([[c1, -s1], [-s1, -c1]]))
    n2 = np.arange(DFT_N2)
    angt = 2.0 * np.pi * (k[:, None] * n2[None, :]) / l
    scale = 1.0 / math.sqrt(l)
    tc = jnp.asarray(np.repeat(np.cos(angt) * scale, LANE, axis=1), F32)
    ts = jnp.asarray(np.repeat(np.sin(angt) * scale, LANE, axis=1), F32)
    ang2 = 2.0 * np.pi * ((n2[:, None] * n2[None, :]) % DFT_N2) / DFT_N2
    c2 = _table(np.cos(ang2))
    s2 = _table(np.sin(ang2))

    a3 = af.reshape(b, n1, DFT_N2 * w)
    b3 = bf.reshape(b, n1, DFT_N2 * w)
    in_blk = pl.BlockSpec((None, n1, DFT_NB * w), lambda bi, j: (bi, 0, j))
    y = pl.pallas_call(
        _dft_stage1_kernel,
        grid=(b, DFT_N2 // DFT_NB),
        in_specs=[
            in_blk, in_blk, _const_spec(m1.shape),
            pl.BlockSpec((n1, DFT_NB * LANE), lambda bi, j: (0, j)),
            pl.BlockSpec((n1, DFT_NB * LANE), lambda bi, j: (0, j)),
        ],
        out_specs=pl.BlockSpec((None, 2 * n1, DFT_NB * w), lambda bi, j: (bi, 0, j)),
        out_shape=jax.ShapeDtypeStruct((b, 2 * n1, DFT_N2 * w), BF16),
        compiler_params=_params("parallel", "parallel"),
        name="dft_stage1",
    )(a3, b3, m1, tc, ts)

    y4 = y.reshape(b, 2 * n1, DFT_N2, w)
    nkb = n1 // DFT_KB
    out = pl.pallas_call(
        _dft_stage2_kernel,
        grid=(b, nkb),
        in_specs=[
            pl.BlockSpec((None, DFT_KB, DFT_N2, w), lambda bi, i: (bi, i, 0, 0)),
            pl.BlockSpec((None, DFT_KB, DFT_N2, w), lambda bi, i: (bi, nkb + i, 0, 0)),
            _const_spec(c2.shape), _const_spec(s2.shape),
        ],
        out_specs=pl.BlockSpec((None, DFT_N2, DFT_KB * w), lambda bi, i: (bi, 0, i)),
        out_shape=jax.ShapeDtypeStruct((b, DFT_N2, n1 * w), BF16),
        compiler_params=_params("parallel", "parallel"),
        name="dft_stage2",
    )(y4, y4, c2, s2)
    return out.reshape(b, l, w)


def _position_dft(af, bf):
    l = af.shape[1]
    if l % (DFT_N2 * SUBLANE_BF16) == 0:
        return _dft_factored(af, bf)
    return _dft_dense(af, bf)


GLA_FACTOR_MAX_DECAY = 80.0


def _gla_chunk(q_ref, k_ref, v_ref, c_ref, r0, s_ref, head_masks, tri_mask, state_mask, pair_sum, forward,
               factored):
    c = GLA_CHUNK
    rows = pl.ds(r0, c)
    q, k, v, cum = q_ref[rows, :], k_ref[rows, :], v_ref[rows, :], c_ref[rows, :]
    tot = cum[c - 1:c] if forward else cum[0:1]
    qe = q * jnp.exp(cum)
    kd = (k * jnp.exp(tot - cum)).astype(BF16)
    if factored:
        ke = (k * jnp.exp(-cum)).astype(BF16)
        q_stack = jnp.concatenate([qe * hm for hm in head_masks], axis=0).astype(BF16)
        att = lax.dot_general(q_stack, ke, _NT, preferred_element_type=F32)
        att = jnp.where(tri_mask, att, 0.0).astype(BF16)
        o_full = _dot(att, v)
        o_intra = jnp.concatenate(
            [o_full[h * c:(h + 1) * c, h * GLA_DV:(h + 1) * GLA_DV] for h in range(GLA_HEADS)], axis=1)
    else:
        ri = lax.broadcasted_iota(jnp.int32, (c, 1), 0)
        v32 = v.astype(F32)

        def row_of(x, j):
            return jnp.sum(jnp.where(ri == j, x, 0.0), axis=0, keepdims=True)

        def key_row(j, o):
            kj, cj, vj = row_of(k, j), row_of(cum, j), row_of(v32, j)
            live = (ri >= j) if forward else (ri <= j)
            e = q * kj * jnp.exp(jnp.where(live, cum - cj, -jnp.inf))
            return o + _dot(e.astype(BF16), pair_sum) * vj

        o_intra = lax.fori_loop(0, c, key_row, jnp.zeros((c, v.shape[1]), F32))
    s = s_ref[...]
    o_inter = lax.dot_general(qe.astype(BF16), s.astype(BF16), _NT, preferred_element_type=F32)
    upd = lax.dot_general(v, kd, _TN, preferred_element_type=F32)
    s_ref[...] = s * jnp.exp(tot) + jnp.where(state_mask, upd, 0.0)
    return o_intra + o_inter


def _gla_kernel(qf_ref, kf_ref, vf_ref, cf_ref, qb_ref, kb_ref, vb_ref, cb_ref, s0f_ref, s0b_ref,
                of_ref, ob_ref, sf_ref, sb_ref, *, factored):
    t = pl.program_id(1)
    c = GLA_CHUNK
    kw = GLA_HEADS * GLA_DK
    vw = GLA_HEADS * GLA_DV

    @pl.when(t == 0)
    def _():
        sf_ref[...] = s0f_ref[...]
        sb_ref[...] = s0b_ref[...]

    lane = lax.broadcasted_iota(jnp.int32, (1, kw), 1)
    head_masks = [(lane // GLA_DK == h).astype(F32) for h in range(GLA_HEADS)]
    ri = lax.broadcasted_iota(jnp.int32, (GLA_HEADS * c, c), 0) % c
    ci = lax.broadcasted_iota(jnp.int32, (GLA_HEADS * c, c), 1)
    tril = ci <= ri
    triu = ci >= ri
    sr = lax.broadcasted_iota(jnp.int32, (vw, kw), 0) // GLA_DV
    scol = lax.broadcasted_iota(jnp.int32, (vw, kw), 1) // GLA_DK
    state_mask = sr == scol
    pr = lax.broadcasted_iota(jnp.int32, (kw, vw), 0) // GLA_DK
    pc = lax.broadcasted_iota(jnp.int32, (kw, vw), 1) // GLA_DV
    pair_sum = (pr == pc).astype(BF16)

    nc = qf_ref.shape[0] // c
    for i in range(nc):
        of_ref[pl.ds(i * c, c), :] = _gla_chunk(
            qf_ref, kf_ref, vf_ref, cf_ref, i * c, sf_ref, head_masks, tril, state_mask, pair_sum, True, factored
        ).astype(of_ref.dtype)
        rb = (nc - 1 - i) * c
        ob_ref[pl.ds(rb, c), :] = _gla_chunk(
            qb_ref, kb_ref, vb_ref, cb_ref, rb, sb_ref, head_masks, triu, state_mask, pair_sum, False, factored
        ).astype(ob_ref.dtype)


def _gla_call(q, k, v, cf, cb, s0f, s0b, factored):
    b, l, kw = q.shape
    vw = v.shape[-1]
    tl = _row_tile(l, 512)
    nt = l // tl
    fwd = lambda w: pl.BlockSpec((None, tl, w), lambda bi, t: (bi, t, 0))
    bwd = lambda w: pl.BlockSpec((None, tl, w), lambda bi, t: (bi, nt - 1 - t, 0))
    st = pl.BlockSpec((None, vw, kw), lambda bi, t: (bi, 0, 0))
    return pl.pallas_call(
        functools.partial(_gla_kernel, factored=factored),
        grid=(b, nt),
        in_specs=[fwd(kw), fwd(kw), fwd(vw), fwd(kw), bwd(kw), bwd(kw), bwd(vw), bwd(kw), st, st],
        out_specs=[fwd(vw), bwd(vw), st, st],
        out_shape=[jax.ShapeDtypeStruct((b, l, vw), BF16), jax.ShapeDtypeStruct((b, l, vw), BF16),
                   jax.ShapeDtypeStruct((b, vw, kw), F32), jax.ShapeDtypeStruct((b, vw, kw), F32)],
        compiler_params=_params("parallel", "arbitrary"),
        name="gla_scan" if factored else "gla_scan_pairwise",
    )(q, k, v, cf, q, k, v, cb, s0f, s0b)


def _gla(q, k, v, cf, cb, dmin, s0f, s0b):
    args = (q, k, v, cf, cb, s0f, s0b)
    return lax.cond(jnp.min(dmin) >= -GLA_FACTOR_MAX_DECAY,
                    lambda: _gla_call(*args, factored=True),
                    lambda: _gla_call(*args, factored=False))


def _mixout_kernel(h_ref, f_ref, of_ref, ob_ref, r_ref, gg_ref, w_ref, ga_ref, o_ref):
    fw = f_ref.shape[-1]
    o = of_ref[...].astype(F32) + ob_ref[...].astype(F32)
    parts = []
    for hd in range(GLA_HEADS):
        oh = o[:, hd * GLA_DV:(hd + 1) * GLA_DV]
        parts.append(oh * lax.rsqrt(jnp.mean(oh * oh, axis=-1, keepdims=True) + EPS))
    on = jnp.concatenate(parts, axis=1) * gg_ref[...]
    on = (on * _silu(r_ref[...].astype(F32))).astype(BF16)
    y = _dot(f_ref[...], w_ref[:fw, :]) + _dot(on, w_ref[fw:, :])
    o_ref[...] = h_ref[...] + ga_ref[...] * y


def _mixout(h, four, of, ob, r, gg, w, ga):
    b, l, d = h.shape
    tm = _row_tile(l, RESIDUAL_TILE_ROWS)
    per_batch = ga.shape[0] == b and b > 1
    mod_idx = (lambda bi, i: (bi, 0, 0)) if per_batch else (lambda bi, i: (0, 0, 0))
    row = lambda w_: pl.BlockSpec((None, tm, w_), lambda bi, i: (bi, i, 0))
    return pl.pallas_call(
        _mixout_kernel,
        grid=(b, l // tm),
        in_specs=[row(d), row(four.shape[-1]), row(of.shape[-1]), row(ob.shape[-1]), row(r.shape[-1]),
                  _const_spec(gg.shape), _const_spec(w.shape), pl.BlockSpec((None, 1, d), mod_idx)],
        out_specs=row(d),
        out_shape=jax.ShapeDtypeStruct((b, l, d), F32),
        compiler_params=_params("parallel", "parallel"),
        name="even_mixout",
    )(h, four, of, ob, r, gg, w, ga)


FFN_HALO = SUBLANE_BF16
FFN_TILE_ROWS = 512
FFN_TF = 256
FFN_PSLOTS = 4
FFN_DOWN_CHUNKS = 2


def _ffn_kernel(h_ref, hp_ref, hn_ref, sc_ref, sh_ref, ga_ref, g_ref, wup_ref, wc_ref, bc_ref, wdn_ref,
                *rest, final):
    if final:
        gfin_ref, o_ref, fext_ref, act_ref, *p_ref = rest
    else:
        o_ref, fext_ref, act_ref, *p_ref = rest
    i = pl.program_id(1)
    last = pl.num_programs(1) - 1
    tm = h_ref.shape[0]
    dff = wdn_ref.shape[0]
    hal = FFN_HALO
    g, sc, sh = g_ref[...], sc_ref[...], sh_ref[...]
    h = h_ref[...]
    fp = _norm_mod(hp_ref[...], g, sc, sh) * (i > 0).astype(F32)
    fn = _norm_mod(hn_ref[...], g, sc, sh) * (i < last).astype(F32)
    fext_ref[0:hal, :] = fp.astype(BF16)
    fext_ref[hal:hal + tm, :] = _norm_mod(h, g, sc, sh).astype(BF16)
    fext_ref[hal + tm:, :] = fn.astype(BF16)
    acc = None
    nch = dff // FFN_TF
    flushed = 0

    def cols_of(c, half):
        return slice(half * dff + c * FFN_TF, half * dff + (c + 1) * FFN_TF)

    def up(c):
        for half in range(2):
            p_ref[(2 * c + half) % FFN_PSLOTS][...] = _dot(fext_ref[...], wup_ref[:, cols_of(c, half)])

    def conv(c, half):
        pr = p_ref[(2 * c + half) % FFN_PSLOTS]
        cs = cols_of(c, half)
        return (pr[pl.ds(hal - 1, tm), :] * wc_ref[0:1, cs]
                + pr[pl.ds(hal, tm), :] * wc_ref[1:2, cs]
                + pr[pl.ds(hal + 1, tm), :] * wc_ref[2:3, cs]
                + bc_ref[:, cs])

    def down(done):
        nonlocal acc, flushed
        ks = slice(flushed * FFN_TF, done * FFN_TF)
        d = _dot(act_ref[:, ks], wdn_ref[ks, :])
        acc = d if acc is None else acc + d
        flushed = done

    up(0)
    for c in range(nch):
        if c + 1 < nch:
            up(c + 1)
        if c - flushed >= FFN_DOWN_CHUNKS:
            down(c)
        act_ref[:, c * FFN_TF:(c + 1) * FFN_TF] = (_silu(conv(c, 0)) * conv(c, 1)).astype(BF16)
    down(nch)
    out = h + ga_ref[...] * acc
    if final:
        out = out * lax.rsqrt(jnp.mean(out * out, axis=-1, keepdims=True) + EPS) * gfin_ref[...]
    o_ref[...] = out


def _conv_ffn(h, sc, sh, ga, g, wup, wc, bc, wdn, gfin=None):
    b, l, d = h.shape
    tm = _row_tile(l, FFN_TILE_ROWS)
    assert tm % FFN_HALO == 0 and wdn.shape[0] % FFN_TF == 0 and CONV_W == 3
    per_batch = ga.shape[0] == b and b > 1
    mod_idx = (lambda bi, i: (bi, 0, 0)) if per_batch else (lambda bi, i: (0, 0, 0))
    hb = tm // FFN_HALO
    nh = l // FFN_HALO
    mod = pl.BlockSpec((None, 1, d), mod_idx)
    in_specs = [
        pl.BlockSpec((None, tm, d), lambda bi, i: (bi, i, 0)),
        pl.BlockSpec((None, FFN_HALO, d), lambda bi, i: (bi, jnp.maximum(i * hb - 1, 0), 0)),
        pl.BlockSpec((None, FFN_HALO, d), lambda bi, i: (bi, jnp.minimum((i + 1) * hb, nh - 1), 0)),
        mod, mod, mod,
        _const_spec(g.shape), _const_spec(wup.shape), _const_spec(wc.shape), _const_spec(bc.shape),
        _const_spec(wdn.shape),
    ]
    args = [h, h, h, sc, sh, ga, g, wup, wc, bc, wdn]
    if gfin is not None:
        in_specs.append(_const_spec(gfin.shape))
        args.append(gfin)
    return pl.pallas_call(
        functools.partial(_ffn_kernel, final=gfin is not None),
        grid=(b, l // tm),
        in_specs=in_specs,
        out_specs=pl.BlockSpec((None, tm, d), lambda bi, i: (bi, i, 0)),
        out_shape=jax.ShapeDtypeStruct((b, l, d), F32),
        scratch_shapes=([pltpu.VMEM((tm + 2 * FFN_HALO, d), BF16), pltpu.VMEM((tm, wdn.shape[0]), BF16)]
                        + [pltpu.VMEM((tm + 2 * FFN_HALO, FFN_TF), F32)] * FFN_PSLOTS),
        compiler_params=_params("parallel", "parallel"),
        name="conv_ffn_final" if gfin is not None else "conv_ffn",
    )(*args)


def _head_norm(t, g):
    return t * lax.rsqrt(jnp.mean(t * t, axis=-1, keepdims=True) + EPS) * g


def _rope(t, cosf, sinf):
    return t * cosf + pltpu.roll(t, HEAD_DIM // 2, 1) * sinf


def _qkv_kernel(h_ref, sc_ref, sh_ref, g_ref, w_ref, gq_ref, gk_ref, *rest, n_q, rope):
    if rope:
        cos_ref, sin_ref = rest[:2]
        rest = rest[2:]
    if n_q:
        q_ref, k_ref, v_ref = rest
    else:
        k_ref, v_ref = rest
    col = lambda j: slice(j * HEAD_DIM, (j + 1) * HEAD_DIM)
    w_q = w_ref.shape[1] - 2 * ATT_KV_HEADS * HEAD_DIM
    c0 = 0 if n_q else w_q
    tm = h_ref.shape[0]
    sub = min(tm, PROJ_SUB_ROWS)

    def project(s):
        rows = slice(s * sub, (s + 1) * sub)
        a = _norm_mod(h_ref[rows, :], g_ref[...], sc_ref[...], sh_ref[...]).astype(BF16)
        return _dot(a, w_ref[:, c0:])

    def finish(s, p):
        rows = slice(s * sub, (s + 1) * sub)
        cosf = cos_ref[rows, :] if rope else None
        sinf = sin_ref[rows, :] if rope else None
        for hd in range(n_q):
            t = _head_norm(p[:, col(hd)], gq_ref[...])
            if rope:
                t = _rope(t, cosf, sinf)
            q_ref[hd, rows, :] = (t * Q_SCALE).astype(BF16)
        for hd in range(ATT_KV_HEADS):
            t = _head_norm(p[:, col(n_q + hd)], gk_ref[...])
            if rope:
                t = _rope(t, cosf, sinf)
            k_ref[hd, rows, :] = t.astype(BF16)
            v_ref[hd, rows, :] = p[:, col(n_q + ATT_KV_HEADS + hd)].astype(BF16)

    p = project(0)
    for s in range(tm // sub):
        p_next = project(s + 1) if (s + 1) * sub < tm else None
        finish(s, p)
        p = p_next


def _qkv(h, sc, sh, g, w, gq, gk, cosf=None, sinf=None, want_q=True):
    b, l, d = h.shape
    tm = _row_tile(l, PROJ_TILE_ROWS)
    n_heads = (w.shape[1] - 2 * ATT_KV_HEADS * HEAD_DIM) // HEAD_DIM
    n_q = n_heads if want_q else 0
    rope = cosf is not None
    per_batch = sc.shape[0] == b and b > 1
    mod_idx = (lambda bi, i: (bi, 0, 0)) if per_batch else (lambda bi, i: (0, 0, 0))
    in_specs = [
        pl.BlockSpec((None, tm, d), lambda bi, i: (bi, i, 0)),
        pl.BlockSpec((None, 1, d), mod_idx), pl.BlockSpec((None, 1, d), mod_idx),
        _const_spec(g.shape), _const_spec(w.shape), _const_spec(gq.shape), _const_spec(gk.shape),
    ]
    args = [h, sc, sh, g, w, gq, gk]
    if rope:
        in_specs += [pl.BlockSpec((tm, HEAD_DIM), lambda bi, i: (i, 0))] * 2
        args += [cosf, sinf]
    heads = lambda n: pl.BlockSpec((None, n, tm, HEAD_DIM), lambda bi, i: (bi, 0, i, 0))
    shape = lambda n: jax.ShapeDtypeStruct((b, n, l, HEAD_DIM), BF16)
    out_specs = [heads(ATT_KV_HEADS), heads(ATT_KV_HEADS)]
    out_shape = [shape(ATT_KV_HEADS), shape(ATT_KV_HEADS)]
    if n_q:
        out_specs = [heads(n_q)] + out_specs
        out_shape = [shape(n_q)] + out_shape
    return pl.pallas_call(
        functools.partial(_qkv_kernel, n_q=n_q, rope=rope),
        grid=(b, l // tm),
        in_specs=in_specs,
        out_specs=out_specs,
        out_shape=out_shape,
        compiler_params=_params("parallel", "parallel"),
        name="qkv_rope" if rope else "kv_ctx",
    )(*args)


def _attn_kernel(q_ref, k_ref, v_ref, o_ref, *scratch, tk, online):
    grp, tq, hd = q_ref.shape
    nlt = tk // LANE
    nk = k_ref.shape[0] // tk
    assert grp % 2 == 0 and hd == LANE
    l_ref, acc_ref = scratch[:grp], scratch[grp:2 * grp]
    p_ref = scratch[2 * grp:2 * grp + 2]
    if online:
        rest = scratch[2 * grp + 2:]
        m_ref, al_ref, s_ref, mx_ref = rest[:grp], rest[grp:2 * grp], rest[2 * grp:2 * grp + 2], rest[2 * grp + 2:]
    for g in range(grp):
        l_ref[g][...] = jnp.zeros(l_ref[g].shape, F32)
        acc_ref[g][...] = jnp.zeros(acc_ref[g].shape, F32)
        if online:
            m_ref[g][...] = jnp.full(m_ref[g].shape, -jnp.inf, F32)

    def chunk(ref, j):
        if isinstance(j, int):
            return ref[j * tk:(j + 1) * tk, :]
        start = pl.multiple_of(jnp.clip(j, 0, nk - 1) * tk, tk)
        return ref[pl.ds(start, tk), :]

    def lane_tiles(x):
        return [x[:, t * LANE:(t + 1) * LANE] for t in range(nlt)]

    def scores(g, j, slot):
        s = lax.dot_general(q_ref[g], chunk(k_ref, j), _NT, preferred_element_type=F32)
        if online:
            s_ref[slot][...] = s
            mx_ref[slot][...] = functools.reduce(jnp.maximum, lane_tiles(s))
        else:
            p = jnp.exp2(s)
            p_ref[slot][...] = p.astype(BF16)
            l_ref[g][...] += functools.reduce(jnp.add, lane_tiles(p))

    def softmax(g, slot):
        m_old = m_ref[g][...]
        m_new = jnp.maximum(m_old, jnp.max(mx_ref[slot][...], axis=-1, keepdims=True))
        alpha = jnp.exp2(m_old - m_new)
        al_ref[g][...] = alpha
        m_ref[g][...] = m_new
        ps = None
        for t in range(nlt):
            p = jnp.exp2(s_ref[slot][:, t * LANE:(t + 1) * LANE] - m_new)
            p_ref[slot][:, t * LANE:(t + 1) * LANE] = p.astype(BF16)
            ps = p if ps is None else ps + p
        l_ref[g][...] = alpha * l_ref[g][...] + ps

    def values(g, j, slot):
        pv = _dot(p_ref[slot][...], chunk(v_ref, j))
        if online:
            acc_ref[g][...] = al_ref[g][...] * acc_ref[g][...] + pv
        else:
            acc_ref[g][...] += pv

    scores(0, 0, 0)
    if online:
        al_ref[grp - 1][...] = jnp.ones(al_ref[grp - 1].shape, F32)
        p_ref[(grp - 1) % 2][...] = jnp.zeros(p_ref[(grp - 1) % 2].shape, BF16)

        def body(j, carry):
            for g in range(grp):
                scores((g + 1) % grp, j + (g + 1) // grp, (g + 1) % 2)
                softmax(g, g % 2)
                values((g - 1) % grp, j + (g - 1) // grp, (g - 1) % 2)
            return carry

        lax.fori_loop(0, nk, body, 0)
        values(grp - 1, nk - 1, (grp - 1) % 2)
    else:
        for j in range(nk):
            for g in range(grp):
                if g + 1 < grp:
                    scores(g + 1, j, (g + 1) % 2)
                elif j + 1 < nk:
                    scores(0, j + 1, 0)
                values(g, j, g % 2)
    for g in range(grp):
        out = acc_ref[g][...] / jnp.sum(l_ref[g][...], axis=-1, keepdims=True)
        o_ref[:, g * hd:(g + 1) * hd] = out.astype(o_ref.dtype)


def _attention(q, k, v, online):
    b, h, s, hd = q.shape
    hkv, lk = k.shape[1], k.shape[2]
    grp = h // hkv
    tq = _row_tile(s, 256)
    tk = next(t for t in (768, 512, 256, 128) if lk % t == 0)
    kv = pl.BlockSpec((None, None, lk, hd), lambda bi, kh, i: (bi, kh, 0, 0))
    scratch = [pltpu.VMEM((tq, LANE), F32)] * (2 * grp) + [pltpu.VMEM((tq, tk), BF16)] * 2
    if online:
        scratch += ([pltpu.VMEM((tq, LANE), F32)] * (2 * grp) + [pltpu.VMEM((tq, tk), F32)] * 2
                    + [pltpu.VMEM((tq, LANE), F32)] * 2)
    return pl.pallas_call(
        functools.partial(_attn_kernel, tk=tk, online=online),
        grid=(b, hkv, s // tq),
        in_specs=[pl.BlockSpec((None, grp, tq, hd), lambda bi, kh, i: (bi, kh, i, 0)), kv, kv],
        out_specs=pl.BlockSpec((None, tq, grp * hd), lambda bi, kh, i: (bi, i, kh)),
        out_shape=jax.ShapeDtypeStruct((b, s, h * hd), BF16),
        scratch_shapes=scratch,
        compiler_params=_params("parallel", "parallel", "arbitrary"),
        name="gqa_flash_online" if online else "gqa_flash",
    )(q, k, v)


def _proj_res_kernel(h_ref, x_ref, w_ref, ga_ref, o_ref):
    o_ref[...] = h_ref[...] + ga_ref[...] * _dot(x_ref[...], w_ref[...])


def _proj_res(h, x, w, ga):
    b, l, d = h.shape
    tm = _row_tile(l, RESIDUAL_TILE_ROWS)
    row = lambda w_: pl.BlockSpec((None, tm, w_), lambda bi, i: (bi, i, 0))
    return pl.pallas_call(
        _proj_res_kernel,
        grid=(b, l // tm),
        in_specs=[row(d), row(x.shape[-1]), _const_spec(w.shape),
                  pl.BlockSpec((None, 1, d), lambda bi, i: (bi, 0, 0))],
        out_specs=row(d),
        out_shape=jax.ShapeDtypeStruct((b, l, d), F32),
        compiler_params=_params("parallel", "parallel"),
        name="att_out",
    )(h, x, w, ga)


def _rope_tables(n_tokens):
    rows = n_tokens // GRID_W
    half = HEAD_DIM // 2
    r, c = jnp.meshgrid(jnp.arange(rows), jnp.arange(GRID_W), indexing='ij')
    inv = ROPE_THETA ** (-jnp.arange(0, half, 2, dtype=F32) / half)
    ang = jnp.concatenate([r.reshape(-1, 1).astype(F32) * inv,
                           c.reshape(-1, 1).astype(F32) * inv], axis=-1)
    cos, sin = jnp.cos(ang), jnp.sin(ang)
    return jnp.concatenate([cos, cos], axis=-1), jnp.concatenate([-sin, sin], axis=-1)


def kernel(x, c, ctx, c_ctx, w_mod, b_mod, g_norm_mix, g_norm_ffn, g_norm_final, w_even_in, w_gla_gate,
           b_gla_gate, g_gla_out, w_even_out, w_qkv, g_q, g_k, w_att_out, w_ffn_up, w_ffn_conv, b_ffn_conv,
           w_ffn_down):
    bsz, seq, d = x.shape
    assert w_mod.shape[0] == 2, "two layers: one even (Fourier || GLA) and one odd (attention)"
    fw = FOURIER_GROUPS * LANE
    kw = GLA_HEADS * GLA_DK
    vw = GLA_HEADS * GLA_DV
    main_w = fw + 2 * kw + 2 * vw

    rows = -(-(bsz + 1) // 8) * 8
    cv = jnp.zeros((rows, d), F32).at[:bsz].set(c).at[bsz].set(c_ctx)
    mod = _modulation(cv, w_mod, b_mod).reshape(2, rows, N_MOD, d)
    lat = lambda i, j: mod[i, :bsz, j][:, None, :]
    cx = lambda i, j: mod[i, bsz:bsz + 1, j][:, None, :]
    row2 = lambda v: v.reshape(1, -1)

    w_in = w_even_in[0]
    wm = w_in[:, :main_w].astype(BF16)
    wz = jnp.zeros((d, LANE), F32).at[:, :GLA_GATE_RANK].set(w_in[:, main_w:]).astype(BF16)
    wg = jnp.zeros((LANE, 2 * kw), F32).at[:GLA_GATE_RANK].set(
        jnp.concatenate([w_gla_gate[0, 0], w_gla_gate[0, 1]], axis=-1)).astype(BF16)
    bg = b_gla_gate[0].reshape(1, 2 * kw)
    nch = np.arange(LANE)
    angc = 2.0 * np.pi * ((nch[:, None] * nch[None, :]) % LANE) / LANE
    cs = _table(np.concatenate([np.cos(angc), np.sin(angc)], axis=1) / math.sqrt(LANE))
    w_out = w_even_out[0].astype(BF16)
    gg = row2(g_gla_out[0])
    ffn_w = [(w_ffn_up[i].astype(BF16), w_ffn_conv[i], row2(b_ffn_conv[i]), w_ffn_down[i].astype(BF16))
             for i in range(2)]

    def even_layer(h, sc1, sh1, ga1, sc2, sh2, ga2, s0f, s0b):
        af, bf, q, k, v, r, cf, cb, dmin = _inproj(h, sc1, sh1, row2(g_norm_mix[0]), wm, wz, cs, wg, bg)
        four = _position_dft(af, bf)
        of, ob, sf, sb = _gla(q, k, v, cf, cb, dmin, s0f, s0b)
        h = _mixout(h, four, of, ob, r, gg, w_out, ga1)
        h = _conv_ffn(h, sc2, sh2, ga2, row2(g_norm_ffn[0]), *ffn_w[0])
        return h, sf, sb

    zero_state = jnp.zeros((bsz, vw, kw), F32)
    h_ctx, s_f, s_b = even_layer(ctx, cx(0, 1), cx(0, 0), cx(0, 2), cx(0, 4), cx(0, 3), cx(0, 5),
                                 zero_state, zero_state)
    h_lat, _, _ = even_layer(x, lat(0, 1), lat(0, 0), lat(0, 2), lat(0, 4), lat(0, 3), lat(0, 5), s_f, s_b)

    wq = w_qkv[0].astype(BF16)
    cosf, sinf = _rope_tables(seq)
    gq, gk, gm = row2(g_q[0]), row2(g_k[0]), row2(g_norm_mix[1])
    k_c, v_c = _qkv(h_ctx, cx(1, 1), cx(1, 0), gm, wq, gq, gk, want_q=False)
    q_l, k_l, v_l = _qkv(h_lat, lat(1, 1), lat(1, 0), gm, wq, gq, gk, cosf, sinf)
    k_all = jnp.concatenate([k_c, k_l], axis=2)
    v_all = jnp.concatenate([v_c, v_l], axis=2)
    score_bound = HEAD_DIM * Q_SCALE * jnp.max(jnp.abs(g_q[0])) * jnp.max(jnp.abs(g_k[0]))
    att = lax.cond(score_bound <= ATT_PLAIN_MAX_LOG2,
                   lambda: _attention(q_l, k_all, v_all, online=False),
                   lambda: _attention(q_l, k_all, v_all, online=True))
    h_lat = _proj_res(h_lat, att, w_att_out[0].astype(BF16), lat(1, 2))
    return _conv_ffn(h_lat, lat(1, 4), lat(1, 3), lat(1, 5), row2(g_norm_ffn[1]), *ffn_w[1],
                     gfin=row2(g_norm_final))
```

```python
import functools
import math

import jax
import jax.numpy as jnp
import numpy as np
from jax import lax
from jax.experimental import pallas as pl
from jax.experimental.pallas import tpu as pltpu

F32 = jnp.float32
BF16 = jnp.bfloat16

EPS = 1e-6
N_MOD = 6
LANE = 128
SUBLANE_BF16 = 16
VMEM_LIMIT = 56 * 1024 * 1024

FOURIER_GROUPS = 4
GLA_HEADS = 4
GLA_DK = 64
GLA_DV = 128
GLA_GATE_RANK = 16
GLA_GATE_TEMP = 16.0
GLA_CHUNK = 64
HEAD_DIM = 128
ATT_KV_HEADS = 2
GRID_W = 64
ROPE_THETA = 10000.0
CONV_W = 3
Q_SCALE = HEAD_DIM ** -0.5 * math.log2(math.e)
ATT_PLAIN_MAX_LOG2 = 64.0

RESIDUAL_TILE_ROWS = 1024
PROJ_TILE_ROWS = 512
PROJ_SUB_ROWS = 256

_NT = (((1,), (1,)), ((), ()))
_TN = (((0,), (0,)), ((), ()))


def _dot(a, b):
    return jnp.dot(a, b, preferred_element_type=F32)


def _params(*sem):
    return pltpu.CompilerParams(dimension_semantics=sem, vmem_limit_bytes=VMEM_LIMIT)


def _const_spec(shape):
    nd = len(shape)
    return pl.BlockSpec(shape, lambda *_: (0,) * nd, pipeline_mode=pl.Buffered(1))


def _table(values):
    return jnp.asarray(values, F32).astype(BF16)


def _row_tile(n, cap):
    t = min(n, cap)
    assert n % t == 0, (n, t)
    return t


def _norm_mod(x, g, sc, sh):
    y = x * lax.rsqrt(jnp.mean(x * x, axis=-1, keepdims=True) + EPS)
    return y * (g * (1.0 + sc)) + sh


def _silu(x):
    return x * jax.nn.sigmoid(x)


def _mod_kernel(cv_ref, w_ref, b_ref, o_ref):
    s = _silu(cv_ref[...]).astype(BF16)
    o_ref[...] = _dot(s, w_ref[...].astype(BF16)) + b_ref[...]


def _modulation(cv, w_mod, b_mod):
    depth, d, n = w_mod.shape
    rows = cv.shape[0]
    tn = 1536
    assert n % tn == 0
    return pl.pallas_call(
        _mod_kernel,
        grid=(depth, n // tn),
        in_specs=[
            pl.BlockSpec((rows, d), lambda i, j: (0, 0)),
            pl.BlockSpec((None, d, tn), lambda i, j: (i, 0, j)),
            pl.BlockSpec((None, 1, tn), lambda i, j: (i, 0, j)),
        ],
        out_specs=pl.BlockSpec((None, rows, tn), lambda i, j: (i, 0, j)),
        out_shape=jax.ShapeDtypeStruct((depth, rows, n), F32),
        compiler_params=_params("parallel", "parallel"),
        name="adaln_mod",
    )(cv, w_mod, b_mod.reshape(depth, 1, n))


def _split3(x):
    hi = x.astype(BF16)
    r1 = x - hi.astype(F32)
    mid = r1.astype(BF16)
    lo = (r1 - mid.astype(F32)).astype(BF16)
    return hi, mid, lo


def _inproj_kernel(h_ref, sc_ref, sh_ref, g_ref, wm_ref, wz_ref, cs_ref, wg_ref, bg_ref,
                   tril_ref, triu_ref,
                   af_ref, bf_ref, q_ref, k_ref, v_ref, r_ref, cf_ref, cb_ref, dmin_ref):
    fw = FOURIER_GROUPS * LANE
    kw = GLA_HEADS * GLA_DK
    vw = GLA_HEADS * GLA_DV
    tm = h_ref.shape[0]
    sub = min(tm, PROJ_SUB_ROWS)
    tril = tril_ref[...]
    triu = triu_ref[...]

    def project(s):
        rows = slice(s * sub, (s + 1) * sub)
        a = _norm_mod(h_ref[rows, :], g_ref[...], sc_ref[...], sh_ref[...]).astype(BF16)
        return _dot(a, wm_ref[...]), _dot(a, wz_ref[...])

    def finish(s, p, z, dmin):
        rows = slice(s * sub, (s + 1) * sub)
        for g in range(FOURIER_GROUPS):
            ab = _dot(p[:, g * LANE:(g + 1) * LANE].astype(BF16), cs_ref[...])
            af_ref[rows, g * LANE:(g + 1) * LANE] = ab[:, :LANE].astype(BF16)
            bf_ref[rows, g * LANE:(g + 1) * LANE] = ab[:, LANE:].astype(BF16)
        q_ref[rows, :] = p[:, fw:fw + kw] * (GLA_DK ** -0.5)
        k_ref[rows, :] = p[:, fw + kw:fw + 2 * kw]
        v_ref[rows, :] = p[:, fw + 2 * kw:fw + 2 * kw + vw].astype(BF16)
        r_ref[rows, :] = p[:, fw + 2 * kw + vw:fw + 2 * kw + 2 * vw].astype(r_ref.dtype)
        zz = _dot(z.astype(BF16), wg_ref[...]) + bg_ref[...]
        loga = (jnp.minimum(zz, 0.0) - jnp.log1p(jnp.exp(-jnp.abs(zz)))) * (1.0 / GLA_GATE_TEMP)
        accf = None
        accb = None
        for term in _split3(loga):
            tf = _dot(tril, term[:, :kw])
            tb = _dot(triu, term[:, kw:])
            accf = tf if accf is None else accf + tf
            accb = tb if accb is None else accb + tb
        cf_ref[rows, :] = accf
        cb_ref[rows, :] = accb
        for c in range(sub // GLA_CHUNK):
            tot = jnp.minimum(accf[(c + 1) * GLA_CHUNK - 1:(c + 1) * GLA_CHUNK],
                              accb[c * GLA_CHUNK:c * GLA_CHUNK + 1])
            dmin = tot if dmin is None else jnp.minimum(dmin, tot)
        return dmin

    dmin = None
    pz = project(0)
    for s in range(tm // sub):
        pz_next = project(s + 1) if (s + 1) * sub < tm else None
        dmin = finish(s, *pz, dmin)
        pz = pz_next
    dmin_ref[...] = jnp.broadcast_to(dmin, dmin_ref.shape)


def _inproj(h, sc, sh, g, wm, wz, cs, wg, bg):
    b, l, d = h.shape
    tm = _row_tile(l, PROJ_TILE_ROWS)
    ic = np.arange(min(tm, PROJ_SUB_ROWS))
    same_chunk = (ic[:, None] // GLA_CHUNK) == (ic[None, :] // GLA_CHUNK)
    tril = jnp.asarray(same_chunk & (ic[:, None] >= ic[None, :]), BF16)
    triu = jnp.asarray(same_chunk & (ic[:, None] <= ic[None, :]), BF16)
    per_batch = sc.shape[0] == b and b > 1
    mod_idx = (lambda bi, i: (bi, 0, 0)) if per_batch else (lambda bi, i: (0, 0, 0))
    row = lambda w: pl.BlockSpec((None, tm, w), lambda bi, i: (bi, i, 0))
    outs = [(512, BF16), (512, BF16), (256, F32), (256, F32), (512, BF16), (512, BF16), (256, F32), (256, F32)]
    kw = GLA_HEADS * GLA_DK
    return pl.pallas_call(
        _inproj_kernel,
        grid=(b, l // tm),
        in_specs=[
            row(d),
            pl.BlockSpec((None, 1, d), mod_idx),
            pl.BlockSpec((None, 1, d), mod_idx),
            _const_spec(g.shape), _const_spec(wm.shape), _const_spec(wz.shape), _const_spec(cs.shape),
            _const_spec(wg.shape), _const_spec(bg.shape), _const_spec(tril.shape), _const_spec(triu.shape),
        ],
        out_specs=[row(w) for w, _ in outs] + [pl.BlockSpec((None, None, 8, kw), lambda bi, i: (bi, i, 0, 0))],
        out_shape=[jax.ShapeDtypeStruct((b, l, w), dt) for w, dt in outs]
        + [jax.ShapeDtypeStruct((b, l // tm, 8, kw), F32)],
        compiler_params=_params("parallel", "parallel"),
        name="even_inproj",
    )(h, sc, sh, g, wm, wz, cs, wg, bg, tril, triu)


def _dft_dense_kernel(a_ref, b_ref, c_ref, s_ref, o_ref):
    o_ref[...] = (_dot(c_ref[...], a_ref[...]) - _dot(s_ref[...], b_ref[...])).astype(o_ref.dtype)


def _dft_dense(af, bf):
    b, l, w = af.shape
    n = np.arange(l)
    ang = 2.0 * np.pi * ((n[:, None] * n[None, :]) % l) / l
    c = _table(np.cos(ang) / math.sqrt(l))
    s = _table(np.sin(ang) / math.sqrt(l))
    blk = pl.BlockSpec((None, l, w), lambda bi: (bi, 0, 0))
    return pl.pallas_call(
        _dft_dense_kernel,
        grid=(b,),
        in_specs=[blk, blk, _const_spec(c.shape), _const_spec(s.shape)],
        out_specs=blk,
        out_shape=jax.ShapeDtypeStruct((b, l, w), BF16),
        compiler_params=_params("parallel"),
        name="dft_dense",
    )(af, bf, c, s)


DFT_N2 = LANE
DFT_NB = 16
DFT_KB = 8


def _dft_stage1_kernel(a_ref, b_ref, m_ref, tc_ref, ts_ref, y_ref):
    n1 = a_ref.shape[0]
    y = _dot(m_ref[...], jnp.concatenate([a_ref[...], b_ref[...]], axis=0))
    for j in range(DFT_NB):
        tc = tc_ref[:, j * LANE:(j + 1) * LANE]
        ts = ts_ref[:, j * LANE:(j + 1) * LANE]
        for g in range(FOURIER_GROUPS):
            sl = slice((j * FOURIER_GROUPS + g) * LANE, (j * FOURIER_GROUPS + g + 1) * LANE)
            yr = y[:n1, sl]
            yi = y[n1:, sl]
            y_ref[:n1, sl] = (yr * tc + yi * ts).astype(BF16)
            y_ref[n1:, sl] = (yi * tc - yr * ts).astype(BF16)


def _dft_stage2_kernel(yr_ref, yi_ref, c_ref, s_ref, o_ref):
    w = yr_ref.shape[-1]
    for kk in range(DFT_KB):
        o = _dot(c_ref[...], yr_ref[kk]) + _dot(s_ref[...], yi_ref[kk])
        o_ref[:, kk * w:(kk + 1) * w] = o.astype(o_ref.dtype)


def _dft_factored(af, bf):
    b, l, w = af.shape
    n1 = l // DFT_N2
    assert l % DFT_N2 == 0 and n1 % SUBLANE_BF16 == 0 and n1 % DFT_KB == 0
    k = np.arange(n1)
    ang1 = 2.0 * np.pi * ((k[:, None] * k[None, :]) % n1) / n1
    c1, s1 = np.cos(ang1), np.sin(ang1)
    m1 = _table(np.block([[c1, -s1], [-s1, -c1]]))
    n2 = np.arange(DFT_N2)
    angt = 2.0 * np.pi * (k[:, None] * n2[None, :]) / l
    scale = 1.0 / math.sqrt(l)
    tc = jnp.asarray(np.repeat(np.cos(angt) * scale, LANE, axis=1), F32)
    ts = jnp.asarray(np.repeat(np.sin(angt) * scale, LANE, axis=1), F32)
    ang2 = 2.0 * np.pi * ((n2[:, None] * n2[None, :]) % DFT_N2) / DFT_N2
    c2 = _table(np.cos(ang2))
    s2 = _table(np.sin(ang2))

    a3 = af.reshape(b, n1, DFT_N2 * w)
    b3 = bf.reshape(b, n1, DFT_N2 * w)
    in_blk = pl.BlockSpec((None, n1, DFT_NB * w), lambda bi, j: (bi, 0, j))
    y = pl.pallas_call(
        _dft_stage1_kernel,
        grid=(b, DFT_N2 // DFT_NB),
        in_specs=[
            in_blk, in_blk, _const_spec(m1.shape),
            pl.BlockSpec((n1, DFT_NB * LANE), lambda bi, j: (0, j)),
            pl.BlockSpec((n1, DFT_NB * LANE), lambda bi, j: (0, j)),
        ],
        out_specs=pl.BlockSpec((None, 2 * n1, DFT_NB * w), lambda bi, j: (bi, 0, j)),
        out_shape=jax.ShapeDtypeStruct((b, 2 * n1, DFT_N2 * w), BF16),
        compiler_params=_params("parallel", "parallel"),
        name="dft_stage1",
    )(a3, b3, m1, tc, ts)

    y4 = y.reshape(b, 2 * n1, DFT_N2, w)
    nkb = n1 // DFT_KB
    out = pl.pallas_call(
        _dft_stage2_kernel,
        grid=(b, nkb),
        in_specs=[
            pl.BlockSpec((None, DFT_KB, DFT_N2, w), lambda bi, i: (bi, i, 0, 0)),
            pl.BlockSpec((None, DFT_KB, DFT_N2, w), lambda bi, i: (bi, nkb + i, 0, 0)),
            _const_spec(c2.shape), _const_spec(s2.shape),
        ],
        out_specs=pl.BlockSpec((None, DFT_N2, DFT_KB * w), lambda bi, i: (bi, 0, i)),
        out_shape=jax.ShapeDtypeStruct((b, DFT_N2, n1 * w), BF16),
        compiler_params=_params("parallel", "parallel"),
        name="dft_stage2",
    )(y4, y4, c2, s2)
    return out.reshape(b, l, w)


def _position_dft(af, bf):
    l = af.shape[1]
    if l % (DFT_N2 * SUBLANE_BF16) == 0:
        return _dft_factored(af, bf)
    return _dft_dense(af, bf)


GLA_FACTOR_MAX_DECAY = 80.0


def _gla_chunk(q_ref, k_ref, v_ref, c_ref, r0, s_ref, head_masks, tri_mask, state_mask, pair_sum, forward,
               factored):
    c = GLA_CHUNK
    rows = pl.ds(r0, c)
    q, k, v, cum = q_ref[rows, :], k_ref[rows, :], v_ref[rows, :], c_ref[rows, :]
    tot = cum[c - 1:c] if forward else cum[0:1]
    qe = q * jnp.exp(cum)
    kd = (k * jnp.exp(tot - cum)).astype(BF16)
    if factored:
        ke = (k * jnp.exp(-cum)).astype(BF16)
        q_stack = jnp.concatenate([qe * hm for hm in head_masks], axis=0).astype(BF16)
        att = lax.dot_general(q_stack, ke, _NT, preferred_element_type=F32)
        att = jnp.where(tri_mask, att, 0.0).astype(BF16)
        o_full = _dot(att, v)
        o_intra = jnp.concatenate(
            [o_full[h * c:(h + 1) * c, h * GLA_DV:(h + 1) * GLA_DV] for h in range(GLA_HEADS)], axis=1)
    else:
        ri = lax.broadcasted_iota(jnp.int32, (c, 1), 0)
        v32 = v.astype(F32)

        def row_of(x, j):
            return jnp.sum(jnp.where(ri == j, x, 0.0), axis=0, keepdims=True)

        def key_row(j, o):
            kj, cj, vj = row_of(k, j), row_of(cum, j), row_of(v32, j)
            live = (ri >= j) if forward else (ri <= j)
            e = q * kj * jnp.exp(jnp.where(live, cum - cj, -jnp.inf))
            return o + _dot(e.astype(BF16), pair_sum) * vj

        o_intra = lax.fori_loop(0, c, key_row, jnp.zeros((c, v.shape[1]), F32))
    s = s_ref[...]
    o_inter = lax.dot_general(qe.astype(BF16), s.astype(BF16), _NT, preferred_element_type=F32)
    upd = lax.dot_general(v, kd, _TN, preferred_element_type=F32)
    s_ref[...] = s * jnp.exp(tot) + jnp.where(state_mask, upd, 0.0)
    return o_intra + o_inter


def _gla_kernel(qf_ref, kf_ref, vf_ref, cf_ref, qb_ref, kb_ref, vb_ref, cb_ref, s0f_ref, s0b_ref,
                of_ref, ob_ref, sf_ref, sb_ref, *, factored):
    t = pl.program_id(1)
    c = GLA_CHUNK
    kw = GLA_HEADS * GLA_DK
    vw = GLA_HEADS * GLA_DV

    @pl.when(t == 0)
    def _():
        sf_ref[...] = s0f_ref[...]
        sb_ref[...] = s0b_ref[...]

    lane = lax.broadcasted_iota(jnp.int32, (1, kw), 1)
    head_masks = [(lane // GLA_DK == h).astype(F32) for h in range(GLA_HEADS)]
    ri = lax.broadcasted_iota(jnp.int32, (GLA_HEADS * c, c), 0) % c
    ci = lax.broadcasted_iota(jnp.int32, (GLA_HEADS * c, c), 1)
    tril = ci <= ri
    triu = ci >= ri
    sr = lax.broadcasted_iota(jnp.int32, (vw, kw), 0) // GLA_DV
    scol = lax.broadcasted_iota(jnp.int32, (vw, kw), 1) // GLA_DK
    state_mask = sr == scol
    pr = lax.broadcasted_iota(jnp.int32, (kw, vw), 0) // GLA_DK
    pc = lax.broadcasted_iota(jnp.int32, (kw, vw), 1) // GLA_DV
    pair_sum = (pr == pc).astype(BF16)

    nc = qf_ref.shape[0] // c
    for i in range(nc):
        of_ref[pl.ds(i * c, c), :] = _gla_chunk(
            qf_ref, kf_ref, vf_ref, cf_ref, i * c, sf_ref, head_masks, tril, state_mask, pair_sum, True, factored
        ).astype(of_ref.dtype)
        rb = (nc - 1 - i) * c
        ob_ref[pl.ds(rb, c), :] = _gla_chunk(
            qb_ref, kb_ref, vb_ref, cb_ref, rb, sb_ref, head_masks, triu, state_mask, pair_sum, False, factored
        ).astype(ob_ref.dtype)


def _gla_call(q, k, v, cf, cb, s0f, s0b, factored):
    b, l, kw = q.shape
    vw = v.shape[-1]
    tl = _row_tile(l, 512)
    nt = l // tl
    fwd = lambda w: pl.BlockSpec((None, tl, w), lambda bi, t: (bi, t, 0))
    bwd = lambda w: pl.BlockSpec((None, tl, w), lambda bi, t: (bi, nt - 1 - t, 0))
    st = pl.BlockSpec((None, vw, kw), lambda bi, t: (bi, 0, 0))
    return pl.pallas_call(
        functools.partial(_gla_kernel, factored=factored),
        grid=(b, nt),
        in_specs=[fwd(kw), fwd(kw), fwd(vw), fwd(kw), bwd(kw), bwd(kw), bwd(vw), bwd(kw), st, st],
        out_specs=[fwd(vw), bwd(vw), st, st],
        out_shape=[jax.ShapeDtypeStruct((b, l, vw), BF16), jax.ShapeDtypeStruct((b, l, vw), BF16),
                   jax.ShapeDtypeStruct((b, vw, kw), F32), jax.ShapeDtypeStruct((b, vw, kw), F32)],
        compiler_params=_params("parallel", "arbitrary"),
        name="gla_scan" if factored else "gla_scan_pairwise",
    )(q, k, v, cf, q, k, v, cb, s0f, s0b)


def _gla(q, k, v, cf, cb, dmin, s0f, s0b):
    args = (q, k, v, cf, cb, s0f, s0b)
    return lax.cond(jnp.min(dmin) >= -GLA_FACTOR_MAX_DECAY,
                    lambda: _gla_call(*args, factored=True),
                    lambda: _gla_call(*args, factored=False))


def _mixout_kernel(h_ref, f_ref, of_ref, ob_ref, r_ref, gg_ref, w_ref, ga_ref, o_ref):
    fw = f_ref.shape[-1]
    o = of_ref[...].astype(F32) + ob_ref[...].astype(F32)
    parts = []
    for hd in range(GLA_HEADS):
        oh = o[:, hd * GLA_DV:(hd + 1) * GLA_DV]
        parts.append(oh * lax.rsqrt(jnp.mean(oh * oh, axis=-1, keepdims=True) + EPS))
    on = jnp.concatenate(parts, axis=1) * gg_ref[...]
    on = (on * _silu(r_ref[...].astype(F32))).astype(BF16)
    y = _dot(f_ref[...], w_ref[:fw, :]) + _dot(on, w_ref[fw:, :])
    o_ref[...] = h_ref[...] + ga_ref[...] * y


def _mixout(h, four, of, ob, r, gg, w, ga):
    b, l, d = h.shape
    tm = _row_tile(l, RESIDUAL_TILE_ROWS)
    per_batch = ga.shape[0] == b and b > 1
    mod_idx = (lambda bi, i: (bi, 0, 0)) if per_batch else (lambda bi, i: (0, 0, 0))
    row = lambda w_: pl.BlockSpec((None, tm, w_), lambda bi, i: (bi, i, 0))
    return pl.pallas_call(
        _mixout_kernel,
        grid=(b, l // tm),
        in_specs=[row(d), row(four.shape[-1]), row(of.shape[-1]), row(ob.shape[-1]), row(r.shape[-1]),
                  _const_spec(gg.shape), _const_spec(w.shape), pl.BlockSpec((None, 1, d), mod_idx)],
        out_specs=row(d),
        out_shape=jax.ShapeDtypeStruct((b, l, d), F32),
        compiler_params=_params("parallel", "parallel"),
        name="even_mixout",
    )(h, four, of, ob, r, gg, w, ga)


FFN_HALO = SUBLANE_BF16
FFN_TILE_ROWS = 512
FFN_TF = 256
FFN_PSLOTS = 4
FFN_DOWN_CHUNKS = 2


def _ffn_kernel(h_ref, hp_ref, hn_ref, sc_ref, sh_ref, ga_ref, g_ref, wup_ref, wc_ref, bc_ref, wdn_ref,
                *rest, final):
    if final:
        gfin_ref, o_ref, fext_ref, act_ref, *p_ref = rest
    else:
        o_ref, fext_ref, act_ref, *p_ref = rest
    i = pl.program_id(1)
    last = pl.num_programs(1) - 1
    tm = h_ref.shape[0]
    dff = wdn_ref.shape[0]
    hal = FFN_HALO
    g, sc, sh = g_ref[...], sc_ref[...], sh_ref[...]
    h = h_ref[...]
    fp = _norm_mod(hp_ref[...], g, sc, sh) * (i > 0).astype(F32)
    fn = _norm_mod(hn_ref[...], g, sc, sh) * (i < last).astype(F32)
    fext_ref[0:hal, :] = fp.astype(BF16)
    fext_ref[hal:hal + tm, :] = _norm_mod(h, g, sc, sh).astype(BF16)
    fext_ref[hal + tm:, :] = fn.astype(BF16)
    acc = None
    nch = dff // FFN_TF
    flushed = 0

    def cols_of(c, half):
        return slice(half * dff + c * FFN_TF, half * dff + (c + 1) * FFN_TF)

    def up(c):
        for half in range(2):
            p_ref[(2 * c + half) % FFN_PSLOTS][...] = _dot(fext_ref[...], wup_ref[:, cols_of(c, half)])

    def conv(c, half):
        p = p_ref[(2 * c + half) % FFN_PSLOTS][...]
        n = p.shape[0]
        cs = cols_of(c, half)
        return (pltpu.roll(p, 1, 0)[hal:hal + tm] * wc_ref[0:1, cs]
                + p[hal:hal + tm] * wc_ref[1:2, cs]
                + pltpu.roll(p, n - 1, 0)[hal:hal + tm] * wc_ref[2:3, cs]
                + bc_ref[:, cs])

    def down(done):
        nonlocal acc, flushed
        ks = slice(flushed * FFN_TF, done * FFN_TF)
        d = _dot(act_ref[:, ks], wdn_ref[ks, :])
        acc = d if acc is None else acc + d
        flushed = done

    up(0)
    for c in range(nch):
        if c + 1 < nch:
            up(c + 1)
        if c - flushed >= FFN_DOWN_CHUNKS:
            down(c)
        act_ref[:, c * FFN_TF:(c + 1) * FFN_TF] = (_silu(conv(c, 0)) * conv(c, 1)).astype(BF16)
    down(nch)
    out = h + ga_ref[...] * acc
    if final:
        out = out * lax.rsqrt(jnp.mean(out * out, axis=-1, keepdims=True) + EPS) * gfin_ref[...]
    o_ref[...] = out


def _conv_ffn(h, sc, sh, ga, g, wup, wc, bc, wdn, gfin=None):
    b, l, d = h.shape
    tm = _row_tile(l, FFN_TILE_ROWS)
    assert tm % FFN_HALO == 0 and wdn.shape[0] % FFN_TF == 0 and CONV_W == 3
    per_batch = ga.shape[0] == b and b > 1
    mod_idx = (lambda bi, i: (bi, 0, 0)) if per_batch else (lambda bi, i: (0, 0, 0))
    hb = tm // FFN_HALO
    nh = l // FFN_HALO
    mod = pl.BlockSpec((None, 1, d), mod_idx)
    in_specs = [
        pl.BlockSpec((None, tm, d), lambda bi, i: (bi, i, 0)),
        pl.BlockSpec((None, FFN_HALO, d), lambda bi, i: (bi, jnp.maximum(i * hb - 1, 0), 0)),
        pl.BlockSpec((None, FFN_HALO, d), lambda bi, i: (bi, jnp.minimum((i + 1) * hb, nh - 1), 0)),
        mod, mod, mod,
        _const_spec(g.shape), _const_spec(wup.shape), _const_spec(wc.shape), _const_spec(bc.shape),
        _const_spec(wdn.shape),
    ]
    args = [h, h, h, sc, sh, ga, g, wup, wc, bc, wdn]
    if gfin is not None:
        in_specs.append(_const_spec(gfin.shape))
        args.append(gfin)
    return pl.pallas_call(
        functools.partial(_ffn_kernel, final=gfin is not None),
        grid=(b, l // tm),
        in_specs=in_specs,
        out_specs=pl.BlockSpec((None, tm, d), lambda bi, i: (bi, i, 0)),
        out_shape=jax.ShapeDtypeStruct((b, l, d), F32),
        scratch_shapes=([pltpu.VMEM((tm + 2 * FFN_HALO, d), BF16), pltpu.VMEM((tm, wdn.shape[0]), BF16)]
                        + [pltpu.VMEM((tm + 2 * FFN_HALO, FFN_TF), F32)] * FFN_PSLOTS),
        compiler_params=_params("parallel", "parallel"),
        name="conv_ffn_final" if gfin is not None else "conv_ffn",
    )(*args)


def _head_norm(t, g):
    return t * lax.rsqrt(jnp.mean(t * t, axis=-1, keepdims=True) + EPS) * g


def _rope(t, cosf, sinf):
    return t * cosf + pltpu.roll(t, HEAD_DIM // 2, 1) * sinf


def _qkv_kernel(h_ref, sc_ref, sh_ref, g_ref, w_ref, gq_ref, gk_ref, *rest, n_q, rope):
    if rope:
        cos_ref, sin_ref = rest[:2]
        rest = rest[2:]
    if n_q:
        q_ref, k_ref, v_ref = rest
    else:
        k_ref, v_ref = rest
    col = lambda j: slice(j * HEAD_DIM, (j + 1) * HEAD_DIM)
    w_q = w_ref.shape[1] - 2 * ATT_KV_HEADS * HEAD_DIM
    c0 = 0 if n_q else w_q
    tm = h_ref.shape[0]
    sub = min(tm, PROJ_SUB_ROWS)

    def project(s):
        rows = slice(s * sub, (s + 1) * sub)
        a = _norm_mod(h_ref[rows, :], g_ref[...], sc_ref[...], sh_ref[...]).astype(BF16)
        return _dot(a, w_ref[:, c0:])

    def finish(s, p):
        rows = slice(s * sub, (s + 1) * sub)
        cosf = cos_ref[rows, :] if rope else None
        sinf = sin_ref[rows, :] if rope else None
        for hd in range(n_q):
            t = _head_norm(p[:, col(hd)], gq_ref[...])
            if rope:
                t = _rope(t, cosf, sinf)
            q_ref[hd, rows, :] = (t * Q_SCALE).astype(BF16)
        for hd in range(ATT_KV_HEADS):
            t = _head_norm(p[:, col(n_q + hd)], gk_ref[...])
            if rope:
                t = _rope(t, cosf, sinf)
            k_ref[hd, rows, :] = t.astype(BF16)
            v_ref[hd, rows, :] = p[:, col(n_q + ATT_KV_HEADS + hd)].astype(BF16)

    p = project(0)
    for s in range(tm // sub):
        p_next = project(s + 1) if (s + 1) * sub < tm else None
        finish(s, p)
        p = p_next


def _qkv(h, sc, sh, g, w, gq, gk, cosf=None, sinf=None, want_q=True):
    b, l, d = h.shape
    tm = _row_tile(l, PROJ_TILE_ROWS)
    n_heads = (w.shape[1] - 2 * ATT_KV_HEADS * HEAD_DIM) // HEAD_DIM
    n_q = n_heads if want_q else 0
    rope = cosf is not None
    per_batch = sc.shape[0] == b and b > 1
    mod_idx = (lambda bi, i: (bi, 0, 0)) if per_batch else (lambda bi, i: (0, 0, 0))
    in_specs = [
        pl.BlockSpec((None, tm, d), lambda bi, i: (bi, i, 0)),
        pl.BlockSpec((None, 1, d), mod_idx), pl.BlockSpec((None, 1, d), mod_idx),
        _const_spec(g.shape), _const_spec(w.shape), _const_spec(gq.shape), _const_spec(gk.shape),
    ]
    args = [h, sc, sh, g, w, gq, gk]
    if rope:
        in_specs += [pl.BlockSpec((tm, HEAD_DIM), lambda bi, i: (i, 0))] * 2
        args += [cosf, sinf]
    heads = lambda n: pl.BlockSpec((None, n, tm, HEAD_DIM), lambda bi, i: (bi, 0, i, 0))
    shape = lambda n: jax.ShapeDtypeStruct((b, n, l, HEAD_DIM), BF16)
    out_specs = [heads(ATT_KV_HEADS), heads(ATT_KV_HEADS)]
    out_shape = [shape(ATT_KV_HEADS), shape(ATT_KV_HEADS)]
    if n_q:
        out_specs = [heads(n_q)] + out_specs
        out_shape = [shape(n_q)] + out_shape
    return pl.pallas_call(
        functools.partial(_qkv_kernel, n_q=n_q, rope=rope),
        grid=(b, l // tm),
        in_specs=in_specs,
        out_specs=out_specs,
        out_shape=out_shape,
        compiler_params=_params("parallel", "parallel"),
        name="qkv_rope" if rope else "kv_ctx",
    )(*args)


def _attn_kernel(q_ref, k_ref, v_ref, o_ref, *scratch, tk, online):
    grp, tq, hd = q_ref.shape
    nlt = tk // LANE
    nk = k_ref.shape[0] // tk
    assert grp % 2 == 0 and hd == LANE
    l_ref, acc_ref = scratch[:grp], scratch[grp:2 * grp]
    p_ref = scratch[2 * grp:2 * grp + 2]
    if online:
        rest = scratch[2 * grp + 2:]
        m_ref, al_ref, s_ref, mx_ref = rest[:grp], rest[grp:2 * grp], rest[2 * grp:2 * grp + 2], rest[2 * grp + 2:]
    for g in range(grp):
        l_ref[g][...] = jnp.zeros(l_ref[g].shape, F32)
        acc_ref[g][...] = jnp.zeros(acc_ref[g].shape, F32)
        if online:
            m_ref[g][...] = jnp.full(m_ref[g].shape, -jnp.inf, F32)

    def chunk(ref, j):
        if isinstance(j, int):
            return ref[j * tk:(j + 1) * tk, :]
        start = pl.multiple_of(jnp.clip(j, 0, nk - 1) * tk, tk)
        return ref[pl.ds(start, tk), :]

    def lane_tiles(x):
        return [x[:, t * LANE:(t + 1) * LANE] for t in range(nlt)]

    def scores(g, j, slot):
        s = lax.dot_general(q_ref[g], chunk(k_ref, j), _NT, preferred_element_type=F32)
        if online:
            s_ref[slot][...] = s
            mx_ref[slot][...] = functools.reduce(jnp.maximum, lane_tiles(s))
        else:
            p = jnp.exp2(s)
            p_ref[slot][...] = p.astype(BF16)
            l_ref[g][...] += functools.reduce(jnp.add, lane_tiles(p))

    def softmax(g, slot):
        m_old = m_ref[g][...]
        m_new = jnp.maximum(m_old, jnp.max(mx_ref[slot][...], axis=-1, keepdims=True))
        alpha = jnp.exp2(m_old - m_new)
        al_ref[g][...] = alpha
        m_ref[g][...] = m_new
        ps = None
        for t in range(nlt):
            p = jnp.exp2(s_ref[slot][:, t * LANE:(t + 1) * LANE] - m_new)
            p_ref[slot][:, t * LANE:(t + 1) * LANE] = p.astype(BF16)
            ps = p if ps is None else ps + p
        l_ref[g][...] = alpha * l_ref[g][...] + ps

    def values(g, j, slot):
        pv = _dot(p_ref[slot][...], chunk(v_ref, j))
        if online:
            acc_ref[g][...] = al_ref[g][...] * acc_ref[g][...] + pv
        else:
            acc_ref[g][...] += pv

    scores(0, 0, 0)
    if online:
        al_ref[grp - 1][...] = jnp.ones(al_ref[grp - 1].shape, F32)
        p_ref[(grp - 1) % 2][...] = jnp.zeros(p_ref[(grp - 1) % 2].shape, BF16)

        def body(j, carry):
            for g in range(grp):
                scores((g + 1) % grp, j + (g + 1) // grp, (g + 1) % 2)
                softmax(g, g % 2)
                values((g - 1) % grp, j + (g - 1) // grp, (g - 1) % 2)
            return carry

        lax.fori_loop(0, nk, body, 0)
        values(grp - 1, nk - 1, (grp - 1) % 2)
    else:
        for j in range(nk):
            for g in range(grp):
                if g + 1 < grp:
                    scores(g + 1, j, (g + 1) % 2)
                elif j + 1 < nk:
                    scores(0, j + 1, 0)
                values(g, j, g % 2)
    for g in range(grp):
        out = acc_ref[g][...] / jnp.sum(l_ref[g][...], axis=-1, keepdims=True)
        o_ref[:, g * hd:(g + 1) * hd] = out.astype(o_ref.dtype)


def _attention(q, k, v, online):
    b, h, s, hd = q.shape
    hkv, lk = k.shape[1], k.shape[2]
    grp = h // hkv
    tq = _row_tile(s, 256)
    tk = next(t for t in (768, 512, 256, 128) if lk % t == 0)
    kv = pl.BlockSpec((None, None, lk, hd), lambda bi, kh, i: (bi, kh, 0, 0))
    scratch = [pltpu.VMEM((tq, LANE), F32)] * (2 * grp) + [pltpu.VMEM((tq, tk), BF16)] * 2
    if online:
        scratch += ([pltpu.VMEM((tq, LANE), F32)] * (2 * grp) + [pltpu.VMEM((tq, tk), F32)] * 2
                    + [pltpu.VMEM((tq, LANE), F32)] * 2)
    return pl.pallas_call(
        functools.partial(_attn_kernel, tk=tk, online=online),
        grid=(b, hkv, s // tq),
        in_specs=[pl.BlockSpec((None, grp, tq, hd), lambda bi, kh, i: (bi, kh, i, 0)), kv, kv],
        out_specs=pl.BlockSpec((None, tq, grp * hd), lambda bi, kh, i: (bi, i, kh)),
        out_shape=jax.ShapeDtypeStruct((b, s, h * hd), BF16),
        scratch_shapes=scratch,
        compiler_params=_params("parallel", "parallel", "arbitrary"),
        name="gqa_flash_online" if online else "gqa_flash",
    )(q, k, v)


def _proj_res_kernel(h_ref, x_ref, w_ref, ga_ref, o_ref):
    o_ref[...] = h_ref[...] + ga_ref[...] * _dot(x_ref[...], w_ref[...])


def _proj_res(h, x, w, ga):
    b, l, d = h.shape
    tm = _row_tile(l, RESIDUAL_TILE_ROWS)
    row = lambda w_: pl.BlockSpec((None, tm, w_), lambda bi, i: (bi, i, 0))
    return pl.pallas_call(
        _proj_res_kernel,
        grid=(b, l // tm),
        in_specs=[row(d), row(x.shape[-1]), _const_spec(w.shape),
                  pl.BlockSpec((None, 1, d), lambda bi, i: (bi, 0, 0))],
        out_specs=row(d),
        out_shape=jax.ShapeDtypeStruct((b, l, d), F32),
        compiler_params=_params("parallel", "parallel"),
        name="att_out",
    )(h, x, w, ga)


def _rope_tables(n_tokens):
    rows = n_tokens // GRID_W
    half = HEAD_DIM // 2
    r, c = jnp.meshgrid(jnp.arange(rows), jnp.arange(GRID_W), indexing='ij')
    inv = ROPE_THETA ** (-jnp.arange(0, half, 2, dtype=F32) / half)
    ang = jnp.concatenate([r.reshape(-1, 1).astype(F32) * inv,
                           c.reshape(-1, 1).astype(F32) * inv], axis=-1)
    cos, sin = jnp.cos(ang), jnp.sin(ang)
    return jnp.concatenate([cos, cos], axis=-1), jnp.concatenate([-sin, sin], axis=-1)


def kernel(x, c, ctx, c_ctx, w_mod, b_mod, g_norm_mix, g_norm_ffn, g_norm_final, w_even_in, w_gla_gate,
           b_gla_gate, g_gla_out, w_even_out, w_qkv, g_q, g_k, w_att_out, w_ffn_up, w_ffn_conv, b_ffn_conv,
           w_ffn_down):
    bsz, seq, d = x.shape
    assert w_mod.shape[0] == 2, "two layers: one even (Fourier || GLA) and one odd (attention)"
    fw = FOURIER_GROUPS * LANE
    kw = GLA_HEADS * GLA_DK
    vw = GLA_HEADS * GLA_DV
    main_w = fw + 2 * kw + 2 * vw

    rows = -(-(bsz + 1) // 8) * 8
    cv = jnp.zeros((rows, d), F32).at[:bsz].set(c).at[bsz].set(c_ctx)
    mod = _modulation(cv, w_mod, b_mod).reshape(2, rows, N_MOD, d)
    lat = lambda i, j: mod[i, :bsz, j][:, None, :]
    cx = lambda i, j: mod[i, bsz:bsz + 1, j][:, None, :]
    row2 = lambda v: v.reshape(1, -1)

    w_in = w_even_in[0]
    wm = w_in[:, :main_w].astype(BF16)
    wz = jnp.zeros((d, LANE), F32).at[:, :GLA_GATE_RANK].set(w_in[:, main_w:]).astype(BF16)
    wg = jnp.zeros((LANE, 2 * kw), F32).at[:GLA_GATE_RANK].set(
        jnp.concatenate([w_gla_gate[0, 0], w_gla_gate[0, 1]], axis=-1)).astype(BF16)
    bg = b_gla_gate[0].reshape(1, 2 * kw)
    nch = np.arange(LANE)
    angc = 2.0 * np.pi * ((nch[:, None] * nch[None, :]) % LANE) / LANE
    cs = _table(np.concatenate([np.cos(angc), np.sin(angc)], axis=1) / math.sqrt(LANE))
    w_out = w_even_out[0].astype(BF16)
    gg = row2(g_gla_out[0])
    ffn_w = [(w_ffn_up[i].astype(BF16), w_ffn_conv[i], row2(b_ffn_conv[i]), w_ffn_down[i].astype(BF16))
             for i in range(2)]

    def even_layer(h, sc1, sh1, ga1, sc2, sh2, ga2, s0f, s0b):
        af, bf, q, k, v, r, cf, cb, dmin = _inproj(h, sc1, sh1, row2(g_norm_mix[0]), wm, wz, cs, wg, bg)
        four = _position_dft(af, bf)
        of, ob, sf, sb = _gla(q, k, v, cf, cb, dmin, s0f, s0b)
        h = _mixout(h, four, of, ob, r, gg, w_out, ga1)
        h = _conv_ffn(h, sc2, sh2, ga2, row2(g_norm_ffn[0]), *ffn_w[0])
        return h, sf, sb

    zero_state = jnp.zeros((bsz, vw, kw), F32)
    h_ctx, s_f, s_b = even_layer(ctx, cx(0, 1), cx(0, 0), cx(0, 2), cx(0, 4), cx(0, 3), cx(0, 5),
                                 zero_state, zero_state)
    h_lat, _, _ = even_layer(x, lat(0, 1), lat(0, 0), lat(0, 2), lat(0, 4), lat(0, 3), lat(0, 5), s_f, s_b)

    wq = w_qkv[0].astype(BF16)
    cosf, sinf = _rope_tables(seq)
    gq, gk, gm = row2(g_q[0]), row2(g_k[0]), row2(g_norm_mix[1])
    k_c, v_c = _qkv(h_ctx, cx(1, 1), cx(1, 0), gm, wq, gq, gk, want_q=False)
    q_l, k_l, v_l = _qkv(h_lat, lat(1, 1), lat(1, 0), gm, wq, gq, gk, cosf, sinf)
    k_all = jnp.concatenate([k_c, k_l], axis=2)
    v_all = jnp.concatenate([v_c, v_l], axis=2)
    score_bound = HEAD_DIM * Q_SCALE * jnp.max(jnp.abs(g_q[0])) * jnp.max(jnp.abs(g_k[0]))
    att = lax.cond(score_bound <= ATT_PLAIN_MAX_LOG2,
                   lambda: _attention(q_l, k_all, v_all, online=False),
                   lambda: _attention(q_l, k_all, v_all, online=True))
    h_lat = _proj_res(h_lat, att, w_att_out[0].astype(BF16), lat(1, 2))
    return _conv_ffn(h_lat, lat(1, 4), lat(1, 3), lat(1, 5), row2(g_norm_ffn[1]), *ffn_w[1],
                     gfin=row2(g_norm_final))
```

```python
import functools
import math

import jax
import jax.numpy as jnp
import numpy as np
from jax import lax
from jax.experimental import pallas as pl
from jax.experimental.pallas import tpu as pltpu

F32 = jnp.float32
BF16 = jnp.bfloat16

EPS = 1e-6
N_MOD = 6
LANE = 128
SUBLANE_BF16 = 16
VMEM_LIMIT = 56 * 1024 * 1024

FOURIER_GROUPS = 4
GLA_HEADS = 4
GLA_DK = 64
GLA_DV = 128
GLA_GATE_RANK = 16
GLA_GATE_TEMP = 16.0
GLA_CHUNK = 64
HEAD_DIM = 128
ATT_KV_HEADS = 2
GRID_W = 64
ROPE_THETA = 10000.0
CONV_W = 3
Q_SCALE = HEAD_DIM ** -0.5 * math.log2(math.e)
ATT_PLAIN_MAX_LOG2 = 64.0

RESIDUAL_TILE_ROWS = 1024
PROJ_TILE_ROWS = 512
PROJ_SUB_ROWS = 256

_NT = (((1,), (1,)), ((), ()))
_TN = (((0,), (0,)), ((), ()))


def _dot(a, b):
    return jnp.dot(a, b, preferred_element_type=F32)


def _params(*sem):
    return pltpu.CompilerParams(dimension_semantics=sem, vmem_limit_bytes=VMEM_LIMIT)


def _const_spec(shape):
    nd = len(shape)
    return pl.BlockSpec(shape, lambda *_: (0,) * nd, pipeline_mode=pl.Buffered(1))


def _table(values):
    return jnp.asarray(values, F32).astype(BF16)


def _row_tile(n, cap):
    t = min(n, cap)
    assert n % t == 0, (n, t)
    return t


def _norm_mod(x, g, sc, sh):
    y = x * lax.rsqrt(jnp.mean(x * x, axis=-1, keepdims=True) + EPS)
    return y * (g * (1.0 + sc)) + sh


def _silu(x):
    return x * jax.nn.sigmoid(x)


def _mod_kernel(cv_ref, w_ref, b_ref, o_ref):
    s = _silu(cv_ref[...]).astype(BF16)
    o_ref[...] = _dot(s, w_ref[...].astype(BF16)) + b_ref[...]


def _modulation(cv, w_mod, b_mod):
    depth, d, n = w_mod.shape
    rows = cv.shape[0]
    tn = 1536
    assert n % tn == 0
    return pl.pallas_call(
        _mod_kernel,
        grid=(depth, n // tn),
        in_specs=[
            pl.BlockSpec((rows, d), lambda i, j: (0, 0)),
            pl.BlockSpec((None, d, tn), lambda i, j: (i, 0, j)),
            pl.BlockSpec((None, 1, tn), lambda i, j: (i, 0, j)),
        ],
        out_specs=pl.BlockSpec((None, rows, tn), lambda i, j: (i, 0, j)),
        out_shape=jax.ShapeDtypeStruct((depth, rows, n), F32),
        compiler_params=_params("parallel", "parallel"),
        name="adaln_mod",
    )(cv, w_mod, b_mod.reshape(depth, 1, n))


def _split3(x):
    hi = x.astype(BF16)
    r1 = x - hi.astype(F32)
    mid = r1.astype(BF16)
    lo = (r1 - mid.astype(F32)).astype(BF16)
    return hi, mid, lo


def _inproj_kernel(h_ref, sc_ref, sh_ref, g_ref, wm_ref, wz_ref, cs_ref, wg_ref, bg_ref,
                   tril_ref, triu_ref,
                   af_ref, bf_ref, q_ref, k_ref, v_ref, r_ref, cf_ref, cb_ref, dmin_ref):
    fw = FOURIER_GROUPS * LANE
    kw = GLA_HEADS * GLA_DK
    vw = GLA_HEADS * GLA_DV
    tm = h_ref.shape[0]
    sub = min(tm, PROJ_SUB_ROWS)
    tril = tril_ref[...]
    triu = triu_ref[...]

    def project(s):
        rows = slice(s * sub, (s + 1) * sub)
        a = _norm_mod(h_ref[rows, :], g_ref[...], sc_ref[...], sh_ref[...]).astype(BF16)
        return _dot(a, wm_ref[...]), _dot(a, wz_ref[...])

    def finish(s, p, z, dmin):
        rows = slice(s * sub, (s + 1) * sub)
        for g in range(FOURIER_GROUPS):
            ab = _dot(p[:, g * LANE:(g + 1) * LANE].astype(BF16), cs_ref[...])
            af_ref[rows, g * LANE:(g + 1) * LANE] = ab[:, :LANE].astype(BF16)
            bf_ref[rows, g * LANE:(g + 1) * LANE] = ab[:, LANE:].astype(BF16)
        q_ref[rows, :] = p[:, fw:fw + kw] * (GLA_DK ** -0.5)
        k_ref[rows, :] = p[:, fw + kw:fw + 2 * kw]
        v_ref[rows, :] = p[:, fw + 2 * kw:fw + 2 * kw + vw].astype(BF16)
        r_ref[rows, :] = p[:, fw + 2 * kw + vw:fw + 2 * kw + 2 * vw].astype(r_ref.dtype)
        zz = _dot(z.astype(BF16), wg_ref[...]) + bg_ref[...]
        loga = (jnp.minimum(zz, 0.0) - jnp.log1p(jnp.exp(-jnp.abs(zz)))) * (1.0 / GLA_GATE_TEMP)
        accf = None
        accb = None
        for term in _split3(loga):
            tf = _dot(tril, term[:, :kw])
            tb = _dot(triu, term[:, kw:])
            accf = tf if accf is None else accf + tf
            accb = tb if accb is None else accb + tb
        cf_ref[rows, :] = accf
        cb_ref[rows, :] = accb
        for c in range(sub // GLA_CHUNK):
            tot = jnp.minimum(accf[(c + 1) * GLA_CHUNK - 1:(c + 1) * GLA_CHUNK],
                              accb[c * GLA_CHUNK:c * GLA_CHUNK + 1])
            dmin = tot if dmin is None else jnp.minimum(dmin, tot)
        return dmin

    dmin = None
    pz = project(0)
    for s in range(tm // sub):
        pz_next = project(s + 1) if (s + 1) * sub < tm else None
        dmin = finish(s, *pz, dmin)
        pz = pz_next
    dmin_ref[...] = jnp.broadcast_to(dmin, dmin_ref.shape)


def _inproj(h, sc, sh, g, wm, wz, cs, wg, bg):
    b, l, d = h.shape
    tm = _row_tile(l, PROJ_TILE_ROWS)
    ic = np.arange(min(tm, PROJ_SUB_ROWS))
    same_chunk = (ic[:, None] // GLA_CHUNK) == (ic[None, :] // GLA_CHUNK)
    tril = jnp.asarray(same_chunk & (ic[:, None] >= ic[None, :]), BF16)
    triu = jnp.asarray(same_chunk & (ic[:, None] <= ic[None, :]), BF16)
    per_batch = sc.shape[0] == b and b > 1
    mod_idx = (lambda bi, i: (bi, 0, 0)) if per_batch else (lambda bi, i: (0, 0, 0))
    row = lambda w: pl.BlockSpec((None, tm, w), lambda bi, i: (bi, i, 0))
    outs = [(512, BF16), (512, BF16), (256, F32), (256, F32), (512, BF16), (512, BF16), (256, F32), (256, F32)]
    kw = GLA_HEADS * GLA_DK
    return pl.pallas_call(
        _inproj_kernel,
        grid=(b, l // tm),
        in_specs=[
            row(d),
            pl.BlockSpec((None, 1, d), mod_idx),
            pl.BlockSpec((None, 1, d), mod_idx),
            _const_spec(g.shape), _const_spec(wm.shape), _const_spec(wz.shape), _const_spec(cs.shape),
            _const_spec(wg.shape), _const_spec(bg.shape), _const_spec(tril.shape), _const_spec(triu.shape),
        ],
        out_specs=[row(w) for w, _ in outs] + [pl.BlockSpec((None, None, 8, kw), lambda bi, i: (bi, i, 0, 0))],
        out_shape=[jax.ShapeDtypeStruct((b, l, w), dt) for w, dt in outs]
        + [jax.ShapeDtypeStruct((b, l // tm, 8, kw), F32)],
        compiler_params=_params("parallel", "parallel"),
        name="even_inproj",
    )(h, sc, sh, g, wm, wz, cs, wg, bg, tril, triu)


def _dft_dense_kernel(a_ref, b_ref, c_ref, s_ref, o_ref):
    o_ref[...] = (_dot(c_ref[...], a_ref[...]) - _dot(s_ref[...], b_ref[...])).astype(o_ref.dtype)


def _dft_dense(af, bf):
    b, l, w = af.shape
    n = np.arange(l)
    ang = 2.0 * np.pi * ((n[:, None] * n[None, :]) % l) / l
    c = _table(np.cos(ang) / math.sqrt(l))
    s = _table(np.sin(ang) / math.sqrt(l))
    blk = pl.BlockSpec((None, l, w), lambda bi: (bi, 0, 0))
    return pl.pallas_call(
        _dft_dense_kernel,
        grid=(b,),
        in_specs=[blk, blk, _const_spec(c.shape), _const_spec(s.shape)],
        out_specs=blk,
        out_shape=jax.ShapeDtypeStruct((b, l, w), BF16),
        compiler_params=_params("parallel"),
        name="dft_dense",
    )(af, bf, c, s)


DFT_N2 = LANE
DFT_NB = 16
DFT_KB = 16


def _dft_stage1_kernel(a_ref, b_ref, m_ref, tc_ref, ts_ref, y_ref):
    n1 = a_ref.shape[0]
    y = _dot(m_ref[...], jnp.concatenate([a_ref[...], b_ref[...]], axis=0))
    for j in range(DFT_NB):
        tc = tc_ref[:, j * LANE:(j + 1) * LANE]
        ts = ts_ref[:, j * LANE:(j + 1) * LANE]
        for g in range(FOURIER_GROUPS):
            sl = slice((j * FOURIER_GROUPS + g) * LANE, (j * FOURIER_GROUPS + g + 1) * LANE)
            yr = y[:n1, sl]
            yi = y[n1:, sl]
            y_ref[:n1, sl] = (yr * tc + yi * ts).astype(BF16)
            y_ref[n1:, sl] = (yi * tc - yr * ts).astype(BF16)


def _dft_stage2_kernel(yr_ref, yi_ref, c_ref, s_ref, o_ref):
    w = yr_ref.shape[-1]
    for kk in range(DFT_KB):
        o = _dot(c_ref[...], yr_ref[kk]) + _dot(s_ref[...], yi_ref[kk])
        o_ref[:, kk * w:(kk + 1) * w] = o.astype(o_ref.dtype)


def _dft_factored(af, bf):
    b, l, w = af.shape
    n1 = l // DFT_N2
    assert l % DFT_N2 == 0 and n1 % SUBLANE_BF16 == 0 and n1 % DFT_KB == 0
    k = np.arange(n1)
    ang1 = 2.0 * np.pi * ((k[:, None] * k[None, :]) % n1) / n1
    c1, s1 = np.cos(ang1), np.sin(ang1)
    m1 = _table(np.block([[c1, -s1], [-s1, -c1]]))
    n2 = np.arange(DFT_N2)
    angt = 2.0 * np.pi * (k[:, None] * n2[None, :]) / l
    scale = 1.0 / math.sqrt(l)
    tc = jnp.asarray(np.repeat(np.cos(angt) * scale, LANE, axis=1), F32)
    ts = jnp.asarray(np.repeat(np.sin(angt) * scale, LANE, axis=1), F32)
    ang2 = 2.0 * np.pi * ((n2[:, None] * n2[None, :]) % DFT_N2) / DFT_N2
    c2 = _table(np.cos(ang2))
    s2 = _table(np.sin(ang2))

    a3 = af.reshape(b, n1, DFT_N2 * w)
    b3 = bf.reshape(b, n1, DFT_N2 * w)
    in_blk = pl.BlockSpec((None, n1, DFT_NB * w), lambda bi, j: (bi, 0, j))
    y = pl.pallas_call(
        _dft_stage1_kernel,
        grid=(b, DFT_N2 // DFT_NB),
        in_specs=[
            in_blk, in_blk, _const_spec(m1.shape),
            pl.BlockSpec((n1, DFT_NB * LANE), lambda bi, j: (0, j)),
            pl.BlockSpec((n1, DFT_NB * LANE), lambda bi, j: (0, j)),
        ],
        out_specs=pl.BlockSpec((None, 2 * n1, DFT_NB * w), lambda bi, j: (bi, 0, j)),
        out_shape=jax.ShapeDtypeStruct((b, 2 * n1, DFT_N2 * w), BF16),
        compiler_params=_params("parallel", "parallel"),
        name="dft_stage1",
    )(a3, b3, m1, tc, ts)

    y4 = y.reshape(b, 2 * n1, DFT_N2, w)
    nkb = n1 // DFT_KB
    out = pl.pallas_call(
        _dft_stage2_kernel,
        grid=(b, nkb),
        in_specs=[
            pl.BlockSpec((None, DFT_KB, DFT_N2, w), lambda bi, i: (bi, i, 0, 0)),
            pl.BlockSpec((None, DFT_KB, DFT_N2, w), lambda bi, i: (bi, nkb + i, 0, 0)),
            _const_spec(c2.shape), _const_spec(s2.shape),
        ],
        out_specs=pl.BlockSpec((None, DFT_N2, DFT_KB * w), lambda bi, i: (bi, 0, i)),
        out_shape=jax.ShapeDtypeStruct((b, DFT_N2, n1 * w), BF16),
        compiler_params=_params("parallel", "parallel"),
        name="dft_stage2",
    )(y4, y4, c2, s2)
    return out.reshape(b, l, w)


def _position_dft(af, bf):
    l = af.shape[1]
    if l % (DFT_N2 * SUBLANE_BF16) == 0:
        return _dft_factored(af, bf)
    return _dft_dense(af, bf)


GLA_FACTOR_MAX_DECAY = 80.0


def _gla_chunk(q_ref, k_ref, v_ref, c_ref, r0, s_ref, head_masks, tri_mask, state_mask, pair_sum, forward,
               factored):
    c = GLA_CHUNK
    rows = pl.ds(r0, c)
    q, k, v, cum = q_ref[rows, :], k_ref[rows, :], v_ref[rows, :], c_ref[rows, :]
    tot = cum[c - 1:c] if forward else cum[0:1]
    qe = q * jnp.exp(cum)
    kd = (k * jnp.exp(tot - cum)).astype(BF16)
    if factored:
        ke = (k * jnp.exp(-cum)).astype(BF16)
        q_stack = jnp.concatenate([qe * hm for hm in head_masks], axis=0).astype(BF16)
        att = lax.dot_general(q_stack, ke, _NT, preferred_element_type=F32)
        att = jnp.where(tri_mask, att, 0.0).astype(BF16)
        o_full = _dot(att, v)
        o_intra = jnp.concatenate(
            [o_full[h * c:(h + 1) * c, h * GLA_DV:(h + 1) * GLA_DV] for h in range(GLA_HEADS)], axis=1)
    else:
        ri = lax.broadcasted_iota(jnp.int32, (c, 1), 0)
        v32 = v.astype(F32)

        def row_of(x, j):
            return jnp.sum(jnp.where(ri == j, x, 0.0), axis=0, keepdims=True)

        def key_row(j, o):
            kj, cj, vj = row_of(k, j), row_of(cum, j), row_of(v32, j)
            live = (ri >= j) if forward else (ri <= j)
            e = q * kj * jnp.exp(jnp.where(live, cum - cj, -jnp.inf))
            return o + _dot(e.astype(BF16), pair_sum) * vj

        o_intra = lax.fori_loop(0, c, key_row, jnp.zeros((c, v.shape[1]), F32))
    s = s_ref[...]
    o_inter = lax.dot_general(qe.astype(BF16), s.astype(BF16), _NT, preferred_element_type=F32)
    upd = lax.dot_general(v, kd, _TN, preferred_element_type=F32)
    s_ref[...] = s * jnp.exp(tot) + jnp.where(state_mask, upd, 0.0)
    return o_intra + o_inter


def _gla_kernel(qf_ref, kf_ref, vf_ref, cf_ref, qb_ref, kb_ref, vb_ref, cb_ref, s0f_ref, s0b_ref,
                of_ref, ob_ref, sf_ref, sb_ref, *, factored):
    t = pl.program_id(1)
    c = GLA_CHUNK
    kw = GLA_HEADS * GLA_DK
    vw = GLA_HEADS * GLA_DV

    @pl.when(t == 0)
    def _():
        sf_ref[...] = s0f_ref[...]
        sb_ref[...] = s0b_ref[...]

    lane = lax.broadcasted_iota(jnp.int32, (1, kw), 1)
    head_masks = [(lane // GLA_DK == h).astype(F32) for h in range(GLA_HEADS)]
    ri = lax.broadcasted_iota(jnp.int32, (GLA_HEADS * c, c), 0) % c
    ci = lax.broadcasted_iota(jnp.int32, (GLA_HEADS * c, c), 1)
    tril = ci <= ri
    triu = ci >= ri
    sr = lax.broadcasted_iota(jnp.int32, (vw, kw), 0) // GLA_DV
    scol = lax.broadcasted_iota(jnp.int32, (vw, kw), 1) // GLA_DK
    state_mask = sr == scol
    pr = lax.broadcasted_iota(jnp.int32, (kw, vw), 0) // GLA_DK
    pc = lax.broadcasted_iota(jnp.int32, (kw, vw), 1) // GLA_DV
    pair_sum = (pr == pc).astype(BF16)

    nc = qf_ref.shape[0] // c
    for i in range(nc):
        of_ref[pl.ds(i * c, c), :] = _gla_chunk(
            qf_ref, kf_ref, vf_ref, cf_ref, i * c, sf_ref, head_masks, tril, state_mask, pair_sum, True, factored
        ).astype(of_ref.dtype)
        rb = (nc - 1 - i) * c
        ob_ref[pl.ds(rb, c), :] = _gla_chunk(
            qb_ref, kb_ref, vb_ref, cb_ref, rb, sb_ref, head_masks, triu, state_mask, pair_sum, False, factored
        ).astype(ob_ref.dtype)


def _gla_call(q, k, v, cf, cb, s0f, s0b, factored):
    b, l, kw = q.shape
    vw = v.shape[-1]
    tl = _row_tile(l, 512)
    nt = l // tl
    fwd = lambda w: pl.BlockSpec((None, tl, w), lambda bi, t: (bi, t, 0))
    bwd = lambda w: pl.BlockSpec((None, tl, w), lambda bi, t: (bi, nt - 1 - t, 0))
    st = pl.BlockSpec((None, vw, kw), lambda bi, t: (bi, 0, 0))
    return pl.pallas_call(
        functools.partial(_gla_kernel, factored=factored),
        grid=(b, nt),
        in_specs=[fwd(kw), fwd(kw), fwd(vw), fwd(kw), bwd(kw), bwd(kw), bwd(vw), bwd(kw), st, st],
        out_specs=[fwd(vw), bwd(vw), st, st],
        out_shape=[jax.ShapeDtypeStruct((b, l, vw), BF16), jax.ShapeDtypeStruct((b, l, vw), BF16),
                   jax.ShapeDtypeStruct((b, vw, kw), F32), jax.ShapeDtypeStruct((b, vw, kw), F32)],
        compiler_params=_params("parallel", "arbitrary"),
        name="gla_scan" if factored else "gla_scan_pairwise",
    )(q, k, v, cf, q, k, v, cb, s0f, s0b)


def _gla(q, k, v, cf, cb, dmin, s0f, s0b):
    args = (q, k, v, cf, cb, s0f, s0b)
    return lax.cond(jnp.min(dmin) >= -GLA_FACTOR_MAX_DECAY,
                    lambda: _gla_call(*args, factored=True),
                    lambda: _gla_call(*args, factored=False))


def _mixout_kernel(h_ref, f_ref, of_ref, ob_ref, r_ref, gg_ref, w_ref, ga_ref, o_ref):
    fw = f_ref.shape[-1]
    o = of_ref[...].astype(F32) + ob_ref[...].astype(F32)
    parts = []
    for hd in range(GLA_HEADS):
        oh = o[:, hd * GLA_DV:(hd + 1) * GLA_DV]
        parts.append(oh * lax.rsqrt(jnp.mean(oh * oh, axis=-1, keepdims=True) + EPS))
    on = jnp.concatenate(parts, axis=1) * gg_ref[...]
    on = (on * _silu(r_ref[...].astype(F32))).astype(BF16)
    y = _dot(f_ref[...], w_ref[:fw, :]) + _dot(on, w_ref[fw:, :])
    o_ref[...] = h_ref[...] + ga_ref[...] * y


def _mixout(h, four, of, ob, r, gg, w, ga):
    b, l, d = h.shape
    tm = _row_tile(l, RESIDUAL_TILE_ROWS)
    per_batch = ga.shape[0] == b and b > 1
    mod_idx = (lambda bi, i: (bi, 0, 0)) if per_batch else (lambda bi, i: (0, 0, 0))
    row = lambda w_: pl.BlockSpec((None, tm, w_), lambda bi, i: (bi, i, 0))
    return pl.pallas_call(
        _mixout_kernel,
        grid=(b, l // tm),
        in_specs=[row(d), row(four.shape[-1]), row(of.shape[-1]), row(ob.shape[-1]), row(r.shape[-1]),
                  _const_spec(gg.shape), _const_spec(w.shape), pl.BlockSpec((None, 1, d), mod_idx)],
        out_specs=row(d),
        out_shape=jax.ShapeDtypeStruct((b, l, d), F32),
        compiler_params=_params("parallel", "parallel"),
        name="even_mixout",
    )(h, four, of, ob, r, gg, w, ga)


FFN_HALO = SUBLANE_BF16
FFN_TILE_ROWS = 512
FFN_TF = 256
FFN_PSLOTS = 4
FFN_DOWN_CHUNKS = 3


def _ffn_kernel(h_ref, hp_ref, hn_ref, sc_ref, sh_ref, ga_ref, g_ref, wup_ref, wc_ref, bc_ref, wdn_ref,
                *rest, final):
    if final:
        gfin_ref, o_ref, fext_ref, act_ref, *p_ref = rest
    else:
        o_ref, fext_ref, act_ref, *p_ref = rest
    i = pl.program_id(1)
    last = pl.num_programs(1) - 1
    tm = h_ref.shape[0]
    dff = wdn_ref.shape[0]
    hal = FFN_HALO
    g, sc, sh = g_ref[...], sc_ref[...], sh_ref[...]
    h = h_ref[...]
    fp = _norm_mod(hp_ref[...], g, sc, sh) * (i > 0).astype(F32)
    fn = _norm_mod(hn_ref[...], g, sc, sh) * (i < last).astype(F32)
    fext_ref[0:hal, :] = fp.astype(BF16)
    fext_ref[hal:hal + tm, :] = _norm_mod(h, g, sc, sh).astype(BF16)
    fext_ref[hal + tm:, :] = fn.astype(BF16)
    acc = None
    nch = dff // FFN_TF
    flushed = 0

    def cols_of(c, half):
        return slice(half * dff + c * FFN_TF, half * dff + (c + 1) * FFN_TF)

    def up(c):
        for half in range(2):
            p_ref[(2 * c + half) % FFN_PSLOTS][...] = _dot(fext_ref[...], wup_ref[:, cols_of(c, half)])

    def conv(c, half):
        p = p_ref[(2 * c + half) % FFN_PSLOTS][...]
        n = p.shape[0]
        cs = cols_of(c, half)
        return (pltpu.roll(p, 1, 0)[hal:hal + tm] * wc_ref[0:1, cs]
                + p[hal:hal + tm] * wc_ref[1:2, cs]
                + pltpu.roll(p, n - 1, 0)[hal:hal + tm] * wc_ref[2:3, cs]
                + bc_ref[:, cs])

    def down(done):
        nonlocal acc, flushed
        ks = slice(flushed * FFN_TF, done * FFN_TF)
        d = _dot(act_ref[:, ks], wdn_ref[ks, :])
        acc = d if acc is None else acc + d
        flushed = done

    up(0)
    for c in range(nch):
        if c + 1 < nch:
            up(c + 1)
        if c - flushed >= FFN_DOWN_CHUNKS:
            down(c)
        act_ref[:, c * FFN_TF:(c + 1) * FFN_TF] = (_silu(conv(c, 0)) * conv(c, 1)).astype(BF16)
    down(nch)
    out = h + ga_ref[...] * acc
    if final:
        out = out * lax.rsqrt(jnp.mean(out * out, axis=-1, keepdims=True) + EPS) * gfin_ref[...]
    o_ref[...] = out


def _conv_ffn(h, sc, sh, ga, g, wup, wc, bc, wdn, gfin=None):
    b, l, d = h.shape
    tm = _row_tile(l, FFN_TILE_ROWS)
    assert tm % FFN_HALO == 0 and wdn.shape[0] % FFN_TF == 0 and CONV_W == 3
    per_batch = ga.shape[0] == b and b > 1
    mod_idx = (lambda bi, i: (bi, 0, 0)) if per_batch else (lambda bi, i: (0, 0, 0))
    hb = tm // FFN_HALO
    nh = l // FFN_HALO
    mod = pl.BlockSpec((None, 1, d), mod_idx)
    in_specs = [
        pl.BlockSpec((None, tm, d), lambda bi, i: (bi, i, 0)),
        pl.BlockSpec((None, FFN_HALO, d), lambda bi, i: (bi, jnp.maximum(i * hb - 1, 0), 0)),
        pl.BlockSpec((None, FFN_HALO, d), lambda bi, i: (bi, jnp.minimum((i + 1) * hb, nh - 1), 0)),
        mod, mod, mod,
        _const_spec(g.shape), _const_spec(wup.shape), _const_spec(wc.shape), _const_spec(bc.shape),
        _const_spec(wdn.shape),
    ]
    args = [h, h, h, sc, sh, ga, g, wup, wc, bc, wdn]
    if gfin is not None:
        in_specs.append(_const_spec(gfin.shape))
        args.append(gfin)
    return pl.pallas_call(
        functools.partial(_ffn_kernel, final=gfin is not None),
        grid=(b, l // tm),
        in_specs=in_specs,
        out_specs=pl.BlockSpec((None, tm, d), lambda bi, i: (bi, i, 0)),
        out_shape=jax.ShapeDtypeStruct((b, l, d), F32),
        scratch_shapes=([pltpu.VMEM((tm + 2 * FFN_HALO, d), BF16), pltpu.VMEM((tm, wdn.shape[0]), BF16)]
                        + [pltpu.VMEM((tm + 2 * FFN_HALO, FFN_TF), F32)] * FFN_PSLOTS),
        compiler_params=_params("parallel", "parallel"),
        name="conv_ffn_final" if gfin is not None else "conv_ffn",
    )(*args)


def _head_norm(t, g):
    return t * lax.rsqrt(jnp.mean(t * t, axis=-1, keepdims=True) + EPS) * g


def _rope(t, cosf, sinf):
    return t * cosf + pltpu.roll(t, HEAD_DIM // 2, 1) * sinf


def _qkv_kernel(h_ref, sc_ref, sh_ref, g_ref, w_ref, gq_ref, gk_ref, *rest, n_q, rope):
    if rope:
        cos_ref, sin_ref = rest[:2]
        rest = rest[2:]
    if n_q:
        q_ref, k_ref, v_ref = rest
    else:
        k_ref, v_ref = rest
    col = lambda j: slice(j * HEAD_DIM, (j + 1) * HEAD_DIM)
    w_q = w_ref.shape[1] - 2 * ATT_KV_HEADS * HEAD_DIM
    c0 = 0 if n_q else w_q
    tm = h_ref.shape[0]
    sub = min(tm, PROJ_SUB_ROWS)

    def project(s):
        rows = slice(s * sub, (s + 1) * sub)
        a = _norm_mod(h_ref[rows, :], g_ref[...], sc_ref[...], sh_ref[...]).astype(BF16)
        return _dot(a, w_ref[:, c0:])

    def finish(s, p):
        rows = slice(s * sub, (s + 1) * sub)
        cosf = cos_ref[rows, :] if rope else None
        sinf = sin_ref[rows, :] if rope else None
        for hd in range(n_q):
            t = _head_norm(p[:, col(hd)], gq_ref[...])
            if rope:
                t = _rope(t, cosf, sinf)
            q_ref[hd, rows, :] = (t * Q_SCALE).astype(BF16)
        for hd in range(ATT_KV_HEADS):
            t = _head_norm(p[:, col(n_q + hd)], gk_ref[...])
            if rope:
                t = _rope(t, cosf, sinf)
            k_ref[hd, rows, :] = t.astype(BF16)
            v_ref[hd, rows, :] = p[:, col(n_q + ATT_KV_HEADS + hd)].astype(BF16)

    p = project(0)
    for s in range(tm // sub):
        p_next = project(s + 1) if (s + 1) * sub < tm else None
        finish(s, p)
        p = p_next


def _qkv(h, sc, sh, g, w, gq, gk, cosf=None, sinf=None, want_q=True):
    b, l, d = h.shape
    tm = _row_tile(l, PROJ_TILE_ROWS)
    n_heads = (w.shape[1] - 2 * ATT_KV_HEADS * HEAD_DIM) // HEAD_DIM
    n_q = n_heads if want_q else 0
    rope = cosf is not None
    per_batch = sc.shape[0] == b and b > 1
    mod_idx = (lambda bi, i: (bi, 0, 0)) if per_batch else (lambda bi, i: (0, 0, 0))
    in_specs = [
        pl.BlockSpec((None, tm, d), lambda bi, i: (bi, i, 0)),
        pl.BlockSpec((None, 1, d), mod_idx), pl.BlockSpec((None, 1, d), mod_idx),
        _const_spec(g.shape), _const_spec(w.shape), _const_spec(gq.shape), _const_spec(gk.shape),
    ]
    args = [h, sc, sh, g, w, gq, gk]
    if rope:
        in_specs += [pl.BlockSpec((tm, HEAD_DIM), lambda bi, i: (i, 0))] * 2
        args += [cosf, sinf]
    heads = lambda n: pl.BlockSpec((None, n, tm, HEAD_DIM), lambda bi, i: (bi, 0, i, 0))
    shape = lambda n: jax.ShapeDtypeStruct((b, n, l, HEAD_DIM), BF16)
    out_specs = [heads(ATT_KV_HEADS), heads(ATT_KV_HEADS)]
    out_shape = [shape(ATT_KV_HEADS), shape(ATT_KV_HEADS)]
    if n_q:
        out_specs = [heads(n_q)] + out_specs
        out_shape = [shape(n_q)] + out_shape
    return pl.pallas_call(
        functools.partial(_qkv_kernel, n_q=n_q, rope=rope),
        grid=(b, l // tm),
        in_specs=in_specs,
        out_specs=out_specs,
        out_shape=out_shape,
        compiler_params=_params("parallel", "parallel"),
        name="qkv_rope" if rope else "kv_ctx",
    )(*args)


def _attn_kernel(q_ref, k_ref, v_ref, o_ref, *scratch, tk, online):
    grp, tq, hd = q_ref.shape
    nlt = tk // LANE
    nk = k_ref.shape[0] // tk
    assert grp % 2 == 0 and hd == LANE
    l_ref, acc_ref = scratch[:grp], scratch[grp:2 * grp]
    p_ref = scratch[2 * grp:2 * grp + 2]
    if online:
        rest = scratch[2 * grp + 2:]
        m_ref, al_ref, s_ref, mx_ref = rest[:grp], rest[grp:2 * grp], rest[2 * grp:2 * grp + 2], rest[2 * grp + 2:]
    for g in range(grp):
        l_ref[g][...] = jnp.zeros(l_ref[g].shape, F32)
        acc_ref[g][...] = jnp.zeros(acc_ref[g].shape, F32)
        if online:
            m_ref[g][...] = jnp.full(m_ref[g].shape, -jnp.inf, F32)

    def chunk(ref, j):
        if isinstance(j, int):
            return ref[j * tk:(j + 1) * tk, :]
        start = pl.multiple_of(jnp.clip(j, 0, nk - 1) * tk, tk)
        return ref[pl.ds(start, tk), :]

    def lane_tiles(x):
        return [x[:, t * LANE:(t + 1) * LANE] for t in range(nlt)]

    def scores(g, j, slot):
        s = lax.dot_general(q_ref[g], chunk(k_ref, j), _NT, preferred_element_type=F32)
        if online:
            s_ref[slot][...] = s
            mx_ref[slot][...] = functools.reduce(jnp.maximum, lane_tiles(s))
        else:
            p = jnp.exp2(s)
            p_ref[slot][...] = p.astype(BF16)
            l_ref[g][...] += functools.reduce(jnp.add, lane_tiles(p))

    def softmax(g, slot):
        m_old = m_ref[g][...]
        m_new = jnp.maximum(m_old, jnp.max(mx_ref[slot][...], axis=-1, keepdims=True))
        alpha = jnp.exp2(m_old - m_new)
        al_ref[g][...] = alpha
        m_ref[g][...] = m_new
        ps = None
        for t in range(nlt):
            p = jnp.exp2(s_ref[slot][:, t * LANE:(t + 1) * LANE] - m_new)
            p_ref[slot][:, t * LANE:(t + 1) * LANE] = p.astype(BF16)
            ps = p if ps is None else ps + p
        l_ref[g][...] = alpha * l_ref[g][...] + ps

    def values(g, j, slot):
        pv = _dot(p_ref[slot][...], chunk(v_ref, j))
        if online:
            acc_ref[g][...] = al_ref[g][...] * acc_ref[g][...] + pv
        else:
            acc_ref[g][...] += pv

    scores(0, 0, 0)
    if online:
        al_ref[grp - 1][...] = jnp.ones(al_ref[grp - 1].shape, F32)
        p_ref[(grp - 1) % 2][...] = jnp.zeros(p_ref[(grp - 1) % 2].shape, BF16)

        def body(j, carry):
            for g in range(grp):
                scores((g + 1) % grp, j + (g + 1) // grp, (g + 1) % 2)
                softmax(g, g % 2)
                values((g - 1) % grp, j + (g - 1) // grp, (g - 1) % 2)
            return carry

        lax.fori_loop(0, nk, body, 0)
        values(grp - 1, nk - 1, (grp - 1) % 2)
    else:
        for j in range(nk):
            for g in range(grp):
                if g + 1 < grp:
                    scores(g + 1, j, (g + 1) % 2)
                elif j + 1 < nk:
                    scores(0, j + 1, 0)
                values(g, j, g % 2)
    for g in range(grp):
        out = acc_ref[g][...] / jnp.sum(l_ref[g][...], axis=-1, keepdims=True)
        o_ref[:, g * hd:(g + 1) * hd] = out.astype(o_ref.dtype)


def _attention(q, k, v, online):
    b, h, s, hd = q.shape
    hkv, lk = k.shape[1], k.shape[2]
    grp = h // hkv
    tq = _row_tile(s, 256)
    tk = next(t for t in (768, 512, 256, 128) if lk % t == 0)
    kv = pl.BlockSpec((None, None, lk, hd), lambda bi, kh, i: (bi, kh, 0, 0))
    scratch = [pltpu.VMEM((tq, LANE), F32)] * (2 * grp) + [pltpu.VMEM((tq, tk), BF16)] * 2
    if online:
        scratch += ([pltpu.VMEM((tq, LANE), F32)] * (2 * grp) + [pltpu.VMEM((tq, tk), F32)] * 2
                    + [pltpu.VMEM((tq, LANE), F32)] * 2)
    return pl.pallas_call(
        functools.partial(_attn_kernel, tk=tk, online=online),
        grid=(b, hkv, s // tq),
        in_specs=[pl.BlockSpec((None, grp, tq, hd), lambda bi, kh, i: (bi, kh, i, 0)), kv, kv],
        out_specs=pl.BlockSpec((None, tq, grp * hd), lambda bi, kh, i: (bi, i, kh)),
        out_shape=jax.ShapeDtypeStruct((b, s, h * hd), BF16),
        scratch_shapes=scratch,
        compiler_params=_params("parallel", "parallel", "arbitrary"),
        name="gqa_flash_online" if online else "gqa_flash",
    )(q, k, v)


def _proj_res_kernel(h_ref, x_ref, w_ref, ga_ref, o_ref):
    o_ref[...] = h_ref[...] + ga_ref[...] * _dot(x_ref[...], w_ref[...])


def _proj_res(h, x, w, ga):
    b, l, d = h.shape
    tm = _row_tile(l, RESIDUAL_TILE_ROWS)
    row = lambda w_: pl.BlockSpec((None, tm, w_), lambda bi, i: (bi, i, 0))
    return pl.pallas_call(
        _proj_res_kernel,
        grid=(b, l // tm),
        in_specs=[row(d), row(x.shape[-1]), _const_spec(w.shape),
                  pl.BlockSpec((None, 1, d), lambda bi, i: (bi, 0, 0))],
        out_specs=row(d),
        out_shape=jax.ShapeDtypeStruct((b, l, d), F32),
        compiler_params=_params("parallel", "parallel"),
        name="att_out",
    )(h, x, w, ga)


def _rope_tables(n_tokens):
    rows = n_tokens // GRID_W
    half = HEAD_DIM // 2
    r, c = jnp.meshgrid(jnp.arange(rows), jnp.arange(GRID_W), indexing='ij')
    inv = ROPE_THETA ** (-jnp.arange(0, half, 2, dtype=F32) / half)
    ang = jnp.concatenate([r.reshape(-1, 1).astype(F32) * inv,
                           c.reshape(-1, 1).astype(F32) * inv], axis=-1)
    cos, sin = jnp.cos(ang), jnp.sin(ang)
    return jnp.concatenate([cos, cos], axis=-1), jnp.concatenate([-sin, sin], axis=-1)


def kernel(x, c, ctx, c_ctx, w_mod, b_mod, g_norm_mix, g_norm_ffn, g_norm_final, w_even_in, w_gla_gate,
           b_gla_gate, g_gla_out, w_even_out, w_qkv, g_q, g_k, w_att_out, w_ffn_up, w_ffn_conv, b_ffn_conv,
           w_ffn_down):
    bsz, seq, d = x.shape
    assert w_mod.shape[0] == 2, "two layers: one even (Fourier || GLA) and one odd (attention)"
    fw = FOURIER_GROUPS * LANE
    kw = GLA_HEADS * GLA_DK
    vw = GLA_HEADS * GLA_DV
    main_w = fw + 2 * kw + 2 * vw

    rows = -(-(bsz + 1) // 8) * 8
    cv = jnp.zeros((rows, d), F32).at[:bsz].set(c).at[bsz].set(c_ctx)
    mod = _modulation(cv, w_mod, b_mod).reshape(2, rows, N_MOD, d)
    lat = lambda i, j: mod[i, :bsz, j][:, None, :]
    cx = lambda i, j: mod[i, bsz:bsz + 1, j][:, None, :]
    row2 = lambda v: v.reshape(1, -1)

    w_in = w_even_in[0]
    wm = w_in[:, :main_w].astype(BF16)
    wz = jnp.zeros((d, LANE), F32).at[:, :GLA_GATE_RANK].set(w_in[:, main_w:]).astype(BF16)
    wg = jnp.zeros((LANE, 2 * kw), F32).at[:GLA_GATE_RANK].set(
        jnp.concatenate([w_gla_gate[0, 0], w_gla_gate[0, 1]], axis=-1)).astype(BF16)
    bg = b_gla_gate[0].reshape(1, 2 * kw)
    nch = np.arange(LANE)
    angc = 2.0 * np.pi * ((nch[:, None] * nch[None, :]) % LANE) / LANE
    cs = _table(np.concatenate([np.cos(angc), np.sin(angc)], axis=1) / math.sqrt(LANE))
    w_out = w_even_out[0].astype(BF16)
    gg = row2(g_gla_out[0])
    ffn_w = [(w_ffn_up[i].astype(BF16), w_ffn_conv[i], row2(b_ffn_conv[i]), w_ffn_down[i].astype(BF16))
             for i in range(2)]

    def even_layer(h, sc1, sh1, ga1, sc2, sh2, ga2, s0f, s0b):
        af, bf, q, k, v, r, cf, cb, dmin = _inproj(h, sc1, sh1, row2(g_norm_mix[0]), wm, wz, cs, wg, bg)
        four = _position_dft(af, bf)
        of, ob, sf, sb = _gla(q, k, v, cf, cb, dmin, s0f, s0b)
        h = _mixout(h, four, of, ob, r, gg, w_out, ga1)
        h = _conv_ffn(h, sc2, sh2, ga2, row2(g_norm_ffn[0]), *ffn_w[0])
        return h, sf, sb

    zero_state = jnp.zeros((bsz, vw, kw), F32)
    h_ctx, s_f, s_b = even_layer(ctx, cx(0, 1), cx(0, 0), cx(0, 2), cx(0, 4), cx(0, 3), cx(0, 5),
                                 zero_state, zero_state)
    h_lat, _, _ = even_layer(x, lat(0, 1), lat(0, 0), lat(0, 2), lat(0, 4), lat(0, 3), lat(0, 5), s_f, s_b)

    wq = w_qkv[0].astype(BF16)
    cosf, sinf = _rope_tables(seq)
    gq, gk, gm = row2(g_q[0]), row2(g_k[0]), row2(g_norm_mix[1])
    k_c, v_c = _qkv(h_ctx, cx(1, 1), cx(1, 0), gm, wq, gq, gk, want_q=False)
    q_l, k_l, v_l = _qkv(h_lat, lat(1, 1), lat(1, 0), gm, wq, gq, gk, cosf, sinf)
    k_all = jnp.concatenate([k_c, k_l], axis=2)
    v_all = jnp.concatenate([v_c, v_l], axis=2)
    score_bound = HEAD_DIM * Q_SCALE * jnp.max(jnp.abs(g_q[0])) * jnp.max(jnp.abs(g_k[0]))
    att = lax.cond(score_bound <= ATT_PLAIN_MAX_LOG2,
                   lambda: _attention(q_l, k_all, v_all, online=False),
                   lambda: _attention(q_l, k_all, v_all, online=True))
    h_lat = _proj_res(h_lat, att, w_att_out[0].astype(BF16), lat(1, 2))
    return _conv_ffn(h_lat, lat(1, 4), lat(1, 3), lat(1, 5), row2(g_norm_ffn[1]), *ffn_w[1],
                     gfin=row2(g_norm_final))
```

```python
import functools
import math

import jax
import jax.numpy as jnp
import numpy as np
from jax import lax
from jax.experimental import pallas as pl
from jax.experimental.pallas import tpu as pltpu

F32 = jnp.float32
BF16 = jnp.bfloat16

EPS = 1e-6
N_MOD = 6
LANE = 128
SUBLANE_BF16 = 16
VMEM_LIMIT = 56 * 1024 * 1024

FOURIER_GROUPS = 4
GLA_HEADS = 4
GLA_DK = 64
GLA_DV = 128
GLA_GATE_RANK = 16
GLA_GATE_TEMP = 16.0
GLA_CHUNK = 64
HEAD_DIM = 128
ATT_KV_HEADS = 2
GRID_W = 64
ROPE_THETA = 10000.0
CONV_W = 3
Q_SCALE = HEAD_DIM ** -0.5 * math.log2(math.e)
ATT_PLAIN_MAX_LOG2 = 64.0

RESIDUAL_TILE_ROWS = 1024
PROJ_TILE_ROWS = 512
PROJ_SUB_ROWS = 256

_NT = (((1,), (1,)), ((), ()))
_TN = (((0,), (0,)), ((), ()))


def _dot(a, b):
    return jnp.dot(a, b, preferred_element_type=F32)


def _params(*sem):
    return pltpu.CompilerParams(dimension_semantics=sem, vmem_limit_bytes=VMEM_LIMIT)


def _const_spec(shape):
    nd = len(shape)
    return pl.BlockSpec(shape, lambda *_: (0,) * nd, pipeline_mode=pl.Buffered(1))


def _table(values):
    return jnp.asarray(values, F32).astype(BF16)


def _row_tile(n, cap):
    t = min(n, cap)
    assert n % t == 0, (n, t)
    return t


def _norm_mod(x, g, sc, sh):
    y = x * lax.rsqrt(jnp.mean(x * x, axis=-1, keepdims=True) + EPS)
    return y * (g * (1.0 + sc)) + sh


def _silu(x):
    return x * jax.nn.sigmoid(x)


def _mod_kernel(cv_ref, w_ref, b_ref, o_ref):
    s = _silu(cv_ref[...]).astype(BF16)
    o_ref[...] = _dot(s, w_ref[...].astype(BF16)) + b_ref[...]


def _modulation(cv, w_mod, b_mod):
    depth, d, n = w_mod.shape
    rows = cv.shape[0]
    tn = 1536
    assert n % tn == 0
    return pl.pallas_call(
        _mod_kernel,
        grid=(depth, n // tn),
        in_specs=[
            pl.BlockSpec((rows, d), lambda i, j: (0, 0)),
            pl.BlockSpec((None, d, tn), lambda i, j: (i, 0, j)),
            pl.BlockSpec((None, 1, tn), lambda i, j: (i, 0, j)),
        ],
        out_specs=pl.BlockSpec((None, rows, tn), lambda i, j: (i, 0, j)),
        out_shape=jax.ShapeDtypeStruct((depth, rows, n), F32),
        compiler_params=_params("parallel", "parallel"),
        name="adaln_mod",
    )(cv, w_mod, b_mod.reshape(depth, 1, n))


def _split3(x):
    hi = x.astype(BF16)
    r1 = x - hi.astype(F32)
    mid = r1.astype(BF16)
    lo = (r1 - mid.astype(F32)).astype(BF16)
    return hi, mid, lo


def _inproj_kernel(h_ref, sc_ref, sh_ref, g_ref, wm_ref, wz_ref, cs_ref, wg_ref, bg_ref,
                   tril_ref, triu_ref,
                   af_ref, bf_ref, q_ref, k_ref, v_ref, r_ref, cf_ref, cb_ref, dmin_ref):
    fw = FOURIER_GROUPS * LANE
    kw = GLA_HEADS * GLA_DK
    vw = GLA_HEADS * GLA_DV
    tm = h_ref.shape[0]
    sub = min(tm, PROJ_SUB_ROWS)
    tril = tril_ref[...]
    triu = triu_ref[...]

    def project(s):
        rows = slice(s * sub, (s + 1) * sub)
        a = _norm_mod(h_ref[rows, :], g_ref[...], sc_ref[...], sh_ref[...]).astype(BF16)
        return _dot(a, wm_ref[...]), _dot(a, wz_ref[...])

    def finish(s, p, z, dmin):
        rows = slice(s * sub, (s + 1) * sub)
        for g in range(FOURIER_GROUPS):
            ab = _dot(p[:, g * LANE:(g + 1) * LANE].astype(BF16), cs_ref[...])
            af_ref[rows, g * LANE:(g + 1) * LANE] = ab[:, :LANE].astype(BF16)
            bf_ref[rows, g * LANE:(g + 1) * LANE] = ab[:, LANE:].astype(BF16)
        q_ref[rows, :] = p[:, fw:fw + kw] * (GLA_DK ** -0.5)
        k_ref[rows, :] = p[:, fw + kw:fw + 2 * kw]
        v_ref[rows, :] = p[:, fw + 2 * kw:fw + 2 * kw + vw].astype(BF16)
        r_ref[rows, :] = p[:, fw + 2 * kw + vw:fw + 2 * kw + 2 * vw].astype(r_ref.dtype)
        zz = _dot(z.astype(BF16), wg_ref[...]) + bg_ref[...]
        loga = (jnp.minimum(zz, 0.0) - jnp.log1p(jnp.exp(-jnp.abs(zz)))) * (1.0 / GLA_GATE_TEMP)
        accf = None
        accb = None
        for term in _split3(loga):
            tf = _dot(tril, term[:, :kw])
            tb = _dot(triu, term[:, kw:])
            accf = tf if accf is None else accf + tf
            accb = tb if accb is None else accb + tb
        cf_ref[rows, :] = accf
        cb_ref[rows, :] = accb
        for c in range(sub // GLA_CHUNK):
            tot = jnp.minimum(accf[(c + 1) * GLA_CHUNK - 1:(c + 1) * GLA_CHUNK],
                              accb[c * GLA_CHUNK:c * GLA_CHUNK + 1])
            dmin = tot if dmin is None else jnp.minimum(dmin, tot)
        return dmin

    dmin = None
    pz = project(0)
    for s in range(tm // sub):
        pz_next = project(s + 1) if (s + 1) * sub < tm else None
        dmin = finish(s, *pz, dmin)
        pz = pz_next
    dmin_ref[...] = jnp.broadcast_to(dmin, dmin_ref.shape)


def _inproj(h, sc, sh, g, wm, wz, cs, wg, bg):
    b, l, d = h.shape
    tm = _row_tile(l, PROJ_TILE_ROWS)
    ic = np.arange(min(tm, PROJ_SUB_ROWS))
    same_chunk = (ic[:, None] // GLA_CHUNK) == (ic[None, :] // GLA_CHUNK)
    tril = jnp.asarray(same_chunk & (ic[:, None] >= ic[None, :]), BF16)
    triu = jnp.asarray(same_chunk & (ic[:, None] <= ic[None, :]), BF16)
    per_batch = sc.shape[0] == b and b > 1
    mod_idx = (lambda bi, i: (bi, 0, 0)) if per_batch else (lambda bi, i: (0, 0, 0))
    row = lambda w: pl.BlockSpec((None, tm, w), lambda bi, i: (bi, i, 0))
    outs = [(512, BF16), (512, BF16), (256, F32), (256, F32), (512, BF16), (512, BF16), (256, F32), (256, F32)]
    kw = GLA_HEADS * GLA_DK
    return pl.pallas_call(
        _inproj_kernel,
        grid=(b, l // tm),
        in_specs=[
            row(d),
            pl.BlockSpec((None, 1, d), mod_idx),
            pl.BlockSpec((None, 1, d), mod_idx),
            _const_spec(g.shape), _const_spec(wm.shape), _const_spec(wz.shape), _const_spec(cs.shape),
            _const_spec(wg.shape), _const_spec(bg.shape), _const_spec(tril.shape), _const_spec(triu.shape),
        ],
        out_specs=[row(w) for w, _ in outs] + [pl.BlockSpec((None, None, 8, kw), lambda bi, i: (bi, i, 0, 0))],
        out_shape=[jax.ShapeDtypeStruct((b, l, w), dt) for w, dt in outs]
        + [jax.ShapeDtypeStruct((b, l // tm, 8, kw), F32)],
        compiler_params=_params("parallel", "parallel"),
        name="even_inproj",
    )(h, sc, sh, g, wm, wz, cs, wg, bg, tril, triu)


def _dft_dense_kernel(a_ref, b_ref, c_ref, s_ref, o_ref):
    o_ref[...] = (_dot(c_ref[...], a_ref[...]) - _dot(s_ref[...], b_ref[...])).astype(o_ref.dtype)


def _dft_dense(af, bf):
    b, l, w = af.shape
    n = np.arange(l)
    ang = 2.0 * np.pi * ((n[:, None] * n[None, :]) % l) / l
    c = _table(np.cos(ang) / math.sqrt(l))
    s = _table(np.sin(ang) / math.sqrt(l))
    blk = pl.BlockSpec((None, l, w), lambda bi: (bi, 0, 0))
    return pl.pallas_call(
        _dft_dense_kernel,
        grid=(b,),
        in_specs=[blk, blk, _const_spec(c.shape), _const_spec(s.shape)],
        out_specs=blk,
        out_shape=jax.ShapeDtypeStruct((b, l, w), BF16),
        compiler_params=_params("parallel"),
        name="dft_dense",
    )(af, bf, c, s)


DFT_N2 = LANE
DFT_NB = 16
DFT_KB = 16


def _dft_stage1_kernel(a_ref, b_ref, m_ref, tc_ref, ts_ref, y_ref):
    n1 = a_ref.shape[0]
    y = _dot(m_ref[...], jnp.concatenate([a_ref[...], b_ref[...]], axis=0))
    for j in range(DFT_NB):
        tc = tc_ref[:, j * LANE:(j + 1) * LANE]
        ts = ts_ref[:, j * LANE:(j + 1) * LANE]
        for g in range(FOURIER_GROUPS):
            sl = slice((j * FOURIER_GROUPS + g) * LANE, (j * FOURIER_GROUPS + g + 1) * LANE)
            yr = y[:n1, sl]
            yi = y[n1:, sl]
            y_ref[:n1, sl] = (yr * tc + yi * ts).astype(BF16)
            y_ref[n1:, sl] = (yi * tc - yr * ts).astype(BF16)


def _dft_stage2_kernel(yr_ref, yi_ref, c_ref, s_ref, o_ref):
    w = yr_ref.shape[-1]
    for kk in range(DFT_KB):
        o = _dot(c_ref[...], yr_ref[kk]) + _dot(s_ref[...], yi_ref[kk])
        o_ref[:, kk * w:(kk + 1) * w] = o.astype(o_ref.dtype)


def _dft_factored(af, bf):
    b, l, w = af.shape
    n1 = l // DFT_N2
    assert l % DFT_N2 == 0 and n1 % SUBLANE_BF16 == 0 and n1 % DFT_KB == 0
    k = np.arange(n1)
    ang1 = 2.0 * np.pi * ((k[:, None] * k[None, :]) % n1) / n1
    c1, s1 = np.cos(ang1), np.sin(ang1)
    m1 = _table(np.block([[c1, -s1], [-s1, -c1]]))
    n2 = np.arange(DFT_N2)
    angt = 2.0 * np.pi * (k[:, None] * n2[None, :]) / l
    scale = 1.0 / math.sqrt(l)
    tc = jnp.asarray(np.repeat(np.cos(angt) * scale, LANE, axis=1), F32)
    ts = jnp.asarray(np.repeat(np.sin(angt) * scale, LANE, axis=1), F32)
    ang2 = 2.0 * np.pi * ((n2[:, None] * n2[None, :]) % DFT_N2) / DFT_N2
    c2 = _table(np.cos(ang2))
    s2 = _table(np.sin(ang2))

    a3 = af.reshape(b, n1, DFT_N2 * w)
    b3 = bf.reshape(b, n1, DFT_N2 * w)
    in_blk = pl.BlockSpec((None, n1, DFT_NB * w), lambda bi, j: (bi, 0, j))
    y = pl.pallas_call(
        _dft_stage1_kernel,
        grid=(b, DFT_N2 // DFT_NB),
        in_specs=[
            in_blk, in_blk, _const_spec(m1.shape),
            pl.BlockSpec((n1, DFT_NB * LANE), lambda bi, j: (0, j)),
            pl.BlockSpec((n1, DFT_NB * LANE), lambda bi, j: (0, j)),
        ],
        out_specs=pl.BlockSpec((None, 2 * n1, DFT_NB * w), lambda bi, j: (bi, 0, j)),
        out_shape=jax.ShapeDtypeStruct((b, 2 * n1, DFT_N2 * w), BF16),
        compiler_params=_params("parallel", "parallel"),
        name="dft_stage1",
    )(a3, b3, m1, tc, ts)

    y4 = y.reshape(b, 2 * n1, DFT_N2, w)
    nkb = n1 // DFT_KB
    out = pl.pallas_call(
        _dft_stage2_kernel,
        grid=(b, nkb),
        in_specs=[
            pl.BlockSpec((None, DFT_KB, DFT_N2, w), lambda bi, i: (bi, i, 0, 0)),
            pl.BlockSpec((None, DFT_KB, DFT_N2, w), lambda bi, i: (bi, nkb + i, 0, 0)),
            _const_spec(c2.shape), _const_spec(s2.shape),
        ],
        out_specs=pl.BlockSpec((None, DFT_N2, DFT_KB * w), lambda bi, i: (bi, 0, i)),
        out_shape=jax.ShapeDtypeStruct((b, DFT_N2, n1 * w), BF16),
        compiler_params=_params("parallel", "parallel"),
        name="dft_stage2",
    )(y4, y4, c2, s2)
    return out.reshape(b, l, w)


def _position_dft(af, bf):
    l = af.shape[1]
    if l % (DFT_N2 * SUBLANE_BF16) == 0:
        return _dft_factored(af, bf)
    return _dft_dense(af, bf)


GLA_FACTOR_MAX_DECAY = 80.0


def _gla_chunk(q_ref, k_ref, v_ref, c_ref, r0, s_ref, head_masks, tri_mask, state_mask, pair_sum, forward,
               factored):
    c = GLA_CHUNK
    rows = pl.ds(r0, c)
    q, k, v, cum = q_ref[rows, :], k_ref[rows, :], v_ref[rows, :], c_ref[rows, :]
    tot = cum[c - 1:c] if forward else cum[0:1]
    qe = q * jnp.exp(cum)
    kd = (k * jnp.exp(tot - cum)).astype(BF16)
    if factored:
        ke = (k * jnp.exp(-cum)).astype(BF16)
        q_stack = jnp.concatenate([qe * hm for hm in head_masks], axis=0).astype(BF16)
        att = lax.dot_general(q_stack, ke, _NT, preferred_element_type=F32)
        att = jnp.where(tri_mask, att, 0.0).astype(BF16)
        o_full = _dot(att, v)
        o_intra = jnp.concatenate(
            [o_full[h * c:(h + 1) * c, h * GLA_DV:(h + 1) * GLA_DV] for h in range(GLA_HEADS)], axis=1)
    else:
        ri = lax.broadcasted_iota(jnp.int32, (c, 1), 0)
        v32 = v.astype(F32)

        def row_of(x, j):
            return jnp.sum(jnp.where(ri == j, x, 0.0), axis=0, keepdims=True)

        def key_row(j, o):
            kj, cj, vj = row_of(k, j), row_of(cum, j), row_of(v32, j)
            live = (ri >= j) if forward else (ri <= j)
            e = q * kj * jnp.exp(jnp.where(live, cum - cj, -jnp.inf))
            return o + _dot(e.astype(BF16), pair_sum) * vj

        o_intra = lax.fori_loop(0, c, key_row, jnp.zeros((c, v.shape[1]), F32))
    s = s_ref[...]
    o_inter = lax.dot_general(qe.astype(BF16), s.astype(BF16), _NT, preferred_element_type=F32)
    upd = lax.dot_general(v, kd, _TN, preferred_element_type=F32)
    s_ref[...] = s * jnp.exp(tot) + jnp.where(state_mask, upd, 0.0)
    return o_intra + o_inter


def _gla_kernel(qf_ref, kf_ref, vf_ref, cf_ref, qb_ref, kb_ref, vb_ref, cb_ref, s0f_ref, s0b_ref,
                of_ref, ob_ref, sf_ref, sb_ref, *, factored):
    t = pl.program_id(1)
    c = GLA_CHUNK
    kw = GLA_HEADS * GLA_DK
    vw = GLA_HEADS * GLA_DV

    @pl.when(t == 0)
    def _():
        sf_ref[...] = s0f_ref[...]
        sb_ref[...] = s0b_ref[...]

    lane = lax.broadcasted_iota(jnp.int32, (1, kw), 1)
    head_masks = [(lane // GLA_DK == h).astype(F32) for h in range(GLA_HEADS)]
    ri = lax.broadcasted_iota(jnp.int32, (GLA_HEADS * c, c), 0) % c
    ci = lax.broadcasted_iota(jnp.int32, (GLA_HEADS * c, c), 1)
    tril = ci <= ri
    triu = ci >= ri
    sr = lax.broadcasted_iota(jnp.int32, (vw, kw), 0) // GLA_DV
    scol = lax.broadcasted_iota(jnp.int32, (vw, kw), 1) // GLA_DK
    state_mask = sr == scol
    pr = lax.broadcasted_iota(jnp.int32, (kw, vw), 0) // GLA_DK
    pc = lax.broadcasted_iota(jnp.int32, (kw, vw), 1) // GLA_DV
    pair_sum = (pr == pc).astype(BF16)

    nc = qf_ref.shape[0] // c
    for i in range(nc):
        of_ref[pl.ds(i * c, c), :] = _gla_chunk(
            qf_ref, kf_ref, vf_ref, cf_ref, i * c, sf_ref, head_masks, tril, state_mask, pair_sum, True, factored
        ).astype(of_ref.dtype)
        rb = (nc - 1 - i) * c
        ob_ref[pl.ds(rb, c), :] = _gla_chunk(
            qb_ref, kb_ref, vb_ref, cb_ref, rb, sb_ref, head_masks, triu, state_mask, pair_sum, False, factored
        ).astype(ob_ref.dtype)


def _gla_call(q, k, v, cf, cb, s0f, s0b, factored):
    b, l, kw = q.shape
    vw = v.shape[-1]
    tl = _row_tile(l, 512)
    nt = l // tl
    fwd = lambda w: pl.BlockSpec((None, tl, w), lambda bi, t: (bi, t, 0))
    bwd = lambda w: pl.BlockSpec((None, tl, w), lambda bi, t: (bi, nt - 1 - t, 0))
    st = pl.BlockSpec((None, vw, kw), lambda bi, t: (bi, 0, 0))
    return pl.pallas_call(
        functools.partial(_gla_kernel, factored=factored),
        grid=(b, nt),
        in_specs=[fwd(kw), fwd(kw), fwd(vw), fwd(kw), bwd(kw), bwd(kw), bwd(vw), bwd(kw), st, st],
        out_specs=[fwd(vw), bwd(vw), st, st],
        out_shape=[jax.ShapeDtypeStruct((b, l, vw), BF16), jax.ShapeDtypeStruct((b, l, vw), BF16),
                   jax.ShapeDtypeStruct((b, vw, kw), F32), jax.ShapeDtypeStruct((b, vw, kw), F32)],
        compiler_params=_params("parallel", "arbitrary"),
        name="gla_scan" if factored else "gla_scan_pairwise",
    )(q, k, v, cf, q, k, v, cb, s0f, s0b)


def _gla(q, k, v, cf, cb, dmin, s0f, s0b):
    args = (q, k, v, cf, cb, s0f, s0b)
    return lax.cond(jnp.min(dmin) >= -GLA_FACTOR_MAX_DECAY,
                    lambda: _gla_call(*args, factored=True),
                    lambda: _gla_call(*args, factored=False))


def _mixout_kernel(h_ref, f_ref, of_ref, ob_ref, r_ref, gg_ref, w_ref, ga_ref, o_ref):
    fw = f_ref.shape[-1]
    o = of_ref[...].astype(F32) + ob_ref[...].astype(F32)
    parts = []
    for hd in range(GLA_HEADS):
        oh = o[:, hd * GLA_DV:(hd + 1) * GLA_DV]
        parts.append(oh * lax.rsqrt(jnp.mean(oh * oh, axis=-1, keepdims=True) + EPS))
    on = jnp.concatenate(parts, axis=1) * gg_ref[...]
    on = (on * _silu(r_ref[...].astype(F32))).astype(BF16)
    y = _dot(f_ref[...], w_ref[:fw, :]) + _dot(on, w_ref[fw:, :])
    o_ref[...] = h_ref[...] + ga_ref[...] * y


def _mixout(h, four, of, ob, r, gg, w, ga):
    b, l, d = h.shape
    tm = _row_tile(l, RESIDUAL_TILE_ROWS)
    per_batch = ga.shape[0] == b and b > 1
    mod_idx = (lambda bi, i: (bi, 0, 0)) if per_batch else (lambda bi, i: (0, 0, 0))
    row = lambda w_: pl.BlockSpec((None, tm, w_), lambda bi, i: (bi, i, 0))
    return pl.pallas_call(
        _mixout_kernel,
        grid=(b, l // tm),
        in_specs=[row(d), row(four.shape[-1]), row(of.shape[-1]), row(ob.shape[-1]), row(r.shape[-1]),
                  _const_spec(gg.shape), _const_spec(w.shape), pl.BlockSpec((None, 1, d), mod_idx)],
        out_specs=row(d),
        out_shape=jax.ShapeDtypeStruct((b, l, d), F32),
        compiler_params=_params("parallel", "parallel"),
        name="even_mixout",
    )(h, four, of, ob, r, gg, w, ga)


FFN_HALO = SUBLANE_BF16
FFN_TILE_ROWS = 512
FFN_TF = 256
FFN_DOWN_CHUNKS = 3


def _ffn_kernel(h_ref, hp_ref, hn_ref, sc_ref, sh_ref, ga_ref, g_ref, wup_ref, wc_ref, bc_ref, wdn_ref,
                *rest, final):
    if final:
        gfin_ref, o_ref, fext_ref, act_ref = rest
    else:
        o_ref, fext_ref, act_ref = rest
    i = pl.program_id(1)
    last = pl.num_programs(1) - 1
    tm = h_ref.shape[0]
    dff = wdn_ref.shape[0]
    hal = FFN_HALO
    g, sc, sh = g_ref[...], sc_ref[...], sh_ref[...]
    h = h_ref[...]
    fp = _norm_mod(hp_ref[...], g, sc, sh) * (i > 0).astype(F32)
    fn = _norm_mod(hn_ref[...], g, sc, sh) * (i < last).astype(F32)
    fext_ref[0:hal, :] = fp.astype(BF16)
    fext_ref[hal:hal + tm, :] = _norm_mod(h, g, sc, sh).astype(BF16)
    fext_ref[hal + tm:, :] = fn.astype(BF16)
    acc = None
    nch = dff // FFN_TF
    flushed = 0

    def cols_of(c, half):
        return slice(half * dff + c * FFN_TF, half * dff + (c + 1) * FFN_TF)

    pvals = {}

    def up(c):
        for half in range(2):
            pvals[(c, half)] = _dot(fext_ref[...], wup_ref[:, cols_of(c, half)])

    def conv(c, half):
        p = pvals.pop((c, half))
        n = p.shape[0]
        cs = cols_of(c, half)
        return (pltpu.roll(p, 1, 0)[hal:hal + tm] * wc_ref[0:1, cs]
                + p[hal:hal + tm] * wc_ref[1:2, cs]
                + pltpu.roll(p, n - 1, 0)[hal:hal + tm] * wc_ref[2:3, cs]
                + bc_ref[:, cs])

    def down(done):
        nonlocal acc, flushed
        ks = slice(flushed * FFN_TF, done * FFN_TF)
        d = _dot(act_ref[:, ks], wdn_ref[ks, :])
        acc = d if acc is None else acc + d
        flushed = done

    up(0)
    for c in range(nch):
        if c + 1 < nch:
            up(c + 1)
        if c - flushed >= FFN_DOWN_CHUNKS or (c == nch - 1 and c > flushed):
            down(c)
        act_ref[:, c * FFN_TF:(c + 1) * FFN_TF] = (_silu(conv(c, 0)) * conv(c, 1)).astype(BF16)
    down(nch)
    out = h + ga_ref[...] * acc
    if final:
        out = out * lax.rsqrt(jnp.mean(out * out, axis=-1, keepdims=True) + EPS) * gfin_ref[...]
    o_ref[...] = out


def _conv_ffn(h, sc, sh, ga, g, wup, wc, bc, wdn, gfin=None):
    b, l, d = h.shape
    tm = _row_tile(l, FFN_TILE_ROWS)
    assert tm % FFN_HALO == 0 and wdn.shape[0] % FFN_TF == 0 and CONV_W == 3
    per_batch = ga.shape[0] == b and b > 1
    mod_idx = (lambda bi, i: (bi, 0, 0)) if per_batch else (lambda bi, i: (0, 0, 0))
    hb = tm // FFN_HALO
    nh = l // FFN_HALO
    mod = pl.BlockSpec((None, 1, d), mod_idx)
    in_specs = [
        pl.BlockSpec((None, tm, d), lambda bi, i: (bi, i, 0)),
        pl.BlockSpec((None, FFN_HALO, d), lambda bi, i: (bi, jnp.maximum(i * hb - 1, 0), 0)),
        pl.BlockSpec((None, FFN_HALO, d), lambda bi, i: (bi, jnp.minimum((i + 1) * hb, nh - 1), 0)),
        mod, mod, mod,
        _const_spec(g.shape), _const_spec(wup.shape), _const_spec(wc.shape), _const_spec(bc.shape),
        _const_spec(wdn.shape),
    ]
    args = [h, h, h, sc, sh, ga, g, wup, wc, bc, wdn]
    if gfin is not None:
        in_specs.append(_const_spec(gfin.shape))
        args.append(gfin)
    return pl.pallas_call(
        functools.partial(_ffn_kernel, final=gfin is not None),
        grid=(b, l // tm),
        in_specs=in_specs,
        out_specs=pl.BlockSpec((None, tm, d), lambda bi, i: (bi, i, 0)),
        out_shape=jax.ShapeDtypeStruct((b, l, d), F32),
        scratch_shapes=[pltpu.VMEM((tm + 2 * FFN_HALO, d), BF16), pltpu.VMEM((tm, wdn.shape[0]), BF16)],
        compiler_params=_params("parallel", "parallel"),
        name="conv_ffn_final" if gfin is not None else "conv_ffn",
    )(*args)


def _head_norm(t, g):
    return t * lax.rsqrt(jnp.mean(t * t, axis=-1, keepdims=True) + EPS) * g


def _rope(t, cosf, sinf):
    return t * cosf + pltpu.roll(t, HEAD_DIM // 2, 1) * sinf


def _qkv_kernel(h_ref, sc_ref, sh_ref, g_ref, w_ref, gq_ref, gk_ref, *rest, n_q, rope):
    if rope:
        cos_ref, sin_ref = rest[:2]
        rest = rest[2:]
    if n_q:
        q_ref, k_ref, v_ref = rest
    else:
        k_ref, v_ref = rest
    col = lambda j: slice(j * HEAD_DIM, (j + 1) * HEAD_DIM)
    w_q = w_ref.shape[1] - 2 * ATT_KV_HEADS * HEAD_DIM
    c0 = 0 if n_q else w_q
    tm = h_ref.shape[0]
    sub = min(tm, PROJ_SUB_ROWS)

    def project(s):
        rows = slice(s * sub, (s + 1) * sub)
        a = _norm_mod(h_ref[rows, :], g_ref[...], sc_ref[...], sh_ref[...]).astype(BF16)
        return _dot(a, w_ref[:, c0:])

    def finish(s, p):
        rows = slice(s * sub, (s + 1) * sub)
        cosf = cos_ref[rows, :] if rope else None
        sinf = sin_ref[rows, :] if rope else None
        for hd in range(n_q):
            t = _head_norm(p[:, col(hd)], gq_ref[...])
            if rope:
                t = _rope(t, cosf, sinf)
            q_ref[hd, rows, :] = (t * Q_SCALE).astype(BF16)
        for hd in range(ATT_KV_HEADS):
            t = _head_norm(p[:, col(n_q + hd)], gk_ref[...])
            if rope:
                t = _rope(t, cosf, sinf)
            k_ref[hd, rows, :] = t.astype(BF16)
            v_ref[hd, rows, :] = p[:, col(n_q + ATT_KV_HEADS + hd)].astype(BF16)

    p = project(0)
    for s in range(tm // sub):
        p_next = project(s + 1) if (s + 1) * sub < tm else None
        finish(s, p)
        p = p_next


def _qkv(h, sc, sh, g, w, gq, gk, cosf=None, sinf=None, want_q=True):
    b, l, d = h.shape
    tm = _row_tile(l, PROJ_TILE_ROWS)
    n_heads = (w.shape[1] - 2 * ATT_KV_HEADS * HEAD_DIM) // HEAD_DIM
    n_q = n_heads if want_q else 0
    rope = cosf is not None
    per_batch = sc.shape[0] == b and b > 1
    mod_idx = (lambda bi, i: (bi, 0, 0)) if per_batch else (lambda bi, i: (0, 0, 0))
    in_specs = [
        pl.BlockSpec((None, tm, d), lambda bi, i: (bi, i, 0)),
        pl.BlockSpec((None, 1, d), mod_idx), pl.BlockSpec((None, 1, d), mod_idx),
        _const_spec(g.shape), _const_spec(w.shape), _const_spec(gq.shape), _const_spec(gk.shape),
    ]
    args = [h, sc, sh, g, w, gq, gk]
    if rope:
        in_specs += [pl.BlockSpec((tm, HEAD_DIM), lambda bi, i: (i, 0))] * 2
        args += [cosf, sinf]
    heads = lambda n: pl.BlockSpec((None, n, tm, HEAD_DIM), lambda bi, i: (bi, 0, i, 0))
    shape = lambda n: jax.ShapeDtypeStruct((b, n, l, HEAD_DIM), BF16)
    out_specs = [heads(ATT_KV_HEADS), heads(ATT_KV_HEADS)]
    out_shape = [shape(ATT_KV_HEADS), shape(ATT_KV_HEADS)]
    if n_q:
        out_specs = [heads(n_q)] + out_specs
        out_shape = [shape(n_q)] + out_shape
    return pl.pallas_call(
        functools.partial(_qkv_kernel, n_q=n_q, rope=rope),
        grid=(b, l // tm),
        in_specs=in_specs,
        out_specs=out_specs,
        out_shape=out_shape,
        compiler_params=_params("parallel", "parallel"),
        name="qkv_rope" if rope else "kv_ctx",
    )(*args)


def _attn_kernel(q_ref, k_ref, v_ref, o_ref, *scratch, tk, online):
    grp, tq, hd = q_ref.shape
    nlt = tk // LANE
    nk = k_ref.shape[0] // tk
    assert grp % 2 == 0 and hd == LANE
    l_ref, acc_ref = scratch[:grp], scratch[grp:2 * grp]
    p_ref = scratch[2 * grp:2 * grp + 2]
    if online:
        rest = scratch[2 * grp + 2:]
        m_ref, al_ref, s_ref, mx_ref = rest[:grp], rest[grp:2 * grp], rest[2 * grp:2 * grp + 2], rest[2 * grp + 2:]
    for g in range(grp):
        l_ref[g][...] = jnp.zeros(l_ref[g].shape, F32)
        acc_ref[g][...] = jnp.zeros(acc_ref[g].shape, F32)
        if online:
            m_ref[g][...] = jnp.full(m_ref[g].shape, -jnp.inf, F32)

    def chunk(ref, j):
        if isinstance(j, int):
            return ref[j * tk:(j + 1) * tk, :]
        start = pl.multiple_of(jnp.clip(j, 0, nk - 1) * tk, tk)
        return ref[pl.ds(start, tk), :]

    def lane_tiles(x):
        return [x[:, t * LANE:(t + 1) * LANE] for t in range(nlt)]

    def scores(g, j, slot):
        s = lax.dot_general(q_ref[g], chunk(k_ref, j), _NT, preferred_element_type=F32)
        if online:
            s_ref[slot][...] = s
            mx_ref[slot][...] = functools.reduce(jnp.maximum, lane_tiles(s))
        else:
            p = jnp.exp2(s)
            p_ref[slot][...] = p.astype(BF16)
            l_ref[g][...] += functools.reduce(jnp.add, lane_tiles(p))

    def softmax(g, slot):
        m_old = m_ref[g][...]
        m_new = jnp.maximum(m_old, jnp.max(mx_ref[slot][...], axis=-1, keepdims=True))
        alpha = jnp.exp2(m_old - m_new)
        al_ref[g][...] = alpha
        m_ref[g][...] = m_new
        ps = None
        for t in range(nlt):
            p = jnp.exp2(s_ref[slot][:, t * LANE:(t + 1) * LANE] - m_new)
            p_ref[slot][:, t * LANE:(t + 1) * LANE] = p.astype(BF16)
            ps = p if ps is None else ps + p
        l_ref[g][...] = alpha * l_ref[g][...] + ps

    def values(g, j, slot):
        pv = _dot(p_ref[slot][...], chunk(v_ref, j))
        if online:
            acc_ref[g][...] = al_ref[g][...] * acc_ref[g][...] + pv
        else:
            acc_ref[g][...] += pv

    scores(0, 0, 0)
    if online:
        al_ref[grp - 1][...] = jnp.ones(al_ref[grp - 1].shape, F32)
        p_ref[(grp - 1) % 2][...] = jnp.zeros(p_ref[(grp - 1) % 2].shape, BF16)

        def body(j, carry):
            for g in range(grp):
                scores((g + 1) % grp, j + (g + 1) // grp, (g + 1) % 2)
                softmax(g, g % 2)
                values((g - 1) % grp, j + (g - 1) // grp, (g - 1) % 2)
            return carry

        lax.fori_loop(0, nk, body, 0)
        values(grp - 1, nk - 1, (grp - 1) % 2)
    else:
        for j in range(nk):
            for g in range(grp):
                if g + 1 < grp:
                    scores(g + 1, j, (g + 1) % 2)
                elif j + 1 < nk:
                    scores(0, j + 1, 0)
                values(g, j, g % 2)
    for g in range(grp):
        out = acc_ref[g][...] / jnp.sum(l_ref[g][...], axis=-1, keepdims=True)
        o_ref[:, g * hd:(g + 1) * hd] = out.astype(o_ref.dtype)


def _attention(q, k, v, online):
    b, h, s, hd = q.shape
    hkv, lk = k.shape[1], k.shape[2]
    grp = h // hkv
    tq = _row_tile(s, 256)
    tk = next(t for t in (768, 512, 256, 128) if lk % t == 0)
    kv = pl.BlockSpec((None, None, lk, hd), lambda bi, kh, i: (bi, kh, 0, 0))
    scratch = [pltpu.VMEM((tq, LANE), F32)] * (2 * grp) + [pltpu.VMEM((tq, tk), BF16)] * 2
    if online:
        scratch += ([pltpu.VMEM((tq, LANE), F32)] * (2 * grp) + [pltpu.VMEM((tq, tk), F32)] * 2
                    + [pltpu.VMEM((tq, LANE), F32)] * 2)
    return pl.pallas_call(
        functools.partial(_attn_kernel, tk=tk, online=online),
        grid=(b, hkv, s // tq),
        in_specs=[pl.BlockSpec((None, grp, tq, hd), lambda bi, kh, i: (bi, kh, i, 0)), kv, kv],
        out_specs=pl.BlockSpec((None, tq, grp * hd), lambda bi, kh, i: (bi, i, kh)),
        out_shape=jax.ShapeDtypeStruct((b, s, h * hd), BF16),
        scratch_shapes=scratch,
        compiler_params=_params("parallel", "parallel", "arbitrary"),
        name="gqa_flash_online" if online else "gqa_flash",
    )(q, k, v)


def _proj_res_kernel(h_ref, x_ref, w_ref, ga_ref, o_ref):
    o_ref[...] = h_ref[...] + ga_ref[...] * _dot(x_ref[...], w_ref[...])


def _proj_res(h, x, w, ga):
    b, l, d = h.shape
    tm = _row_tile(l, RESIDUAL_TILE_ROWS)
    row = lambda w_: pl.BlockSpec((None, tm, w_), lambda bi, i: (bi, i, 0))
    return pl.pallas_call(
        _proj_res_kernel,
        grid=(b, l // tm),
        in_specs=[row(d), row(x.shape[-1]), _const_spec(w.shape),
                  pl.BlockSpec((None, 1, d), lambda bi, i: (bi, 0, 0))],
        out_specs=row(d),
        out_shape=jax.ShapeDtypeStruct((b, l, d), F32),
        compiler_params=_params("parallel", "parallel"),
        name="att_out",
    )(h, x, w, ga)


def _rope_tables(n_tokens):
    rows = n_tokens // GRID_W
    half = HEAD_DIM // 2
    r, c = jnp.meshgrid(jnp.arange(rows), jnp.arange(GRID_W), indexing='ij')
    inv = ROPE_THETA ** (-jnp.arange(0, half, 2, dtype=F32) / half)
    ang = jnp.concatenate([r.reshape(-1, 1).astype(F32) * inv,
                           c.reshape(-1, 1).astype(F32) * inv], axis=-1)
    cos, sin = jnp.cos(ang), jnp.sin(ang)
    return jnp.concatenate([cos, cos], axis=-1), jnp.concatenate([-sin, sin], axis=-1)


def kernel(x, c, ctx, c_ctx, w_mod, b_mod, g_norm_mix, g_norm_ffn, g_norm_final, w_even_in, w_gla_gate,
           b_gla_gate, g_gla_out, w_even_out, w_qkv, g_q, g_k, w_att_out, w_ffn_up, w_ffn_conv, b_ffn_conv,
           w_ffn_down):
    bsz, seq, d = x.shape
    assert w_mod.shape[0] == 2, "two layers: one even (Fourier || GLA) and one odd (attention)"
    fw = FOURIER_GROUPS * LANE
    kw = GLA_HEADS * GLA_DK
    vw = GLA_HEADS * GLA_DV
    main_w = fw + 2 * kw + 2 * vw

    rows = -(-(bsz + 1) // 8) * 8
    cv = jnp.zeros((rows, d), F32).at[:bsz].set(c).at[bsz].set(c_ctx)
    mod = _modulation(cv, w_mod, b_mod).reshape(2, rows, N_MOD, d)
    lat = lambda i, j: mod[i, :bsz, j][:, None, :]
    cx = lambda i, j: mod[i, bsz:bsz + 1, j][:, None, :]
    row2 = lambda v: v.reshape(1, -1)

    w_in = w_even_in[0]
    wm = w_in[:, :main_w].astype(BF16)
    wz = jnp.zeros((d, LANE), F32).at[:, :GLA_GATE_RANK].set(w_in[:, main_w:]).astype(BF16)
    wg = jnp.zeros((LANE, 2 * kw), F32).at[:GLA_GATE_RANK].set(
        jnp.concatenate([w_gla_gate[0, 0], w_gla_gate[0, 1]], axis=-1)).astype(BF16)
    bg = b_gla_gate[0].reshape(1, 2 * kw)
    nch = np.arange(LANE)
    angc = 2.0 * np.pi * ((nch[:, None] * nch[None, :]) % LANE) / LANE
    cs = _table(np.concatenate([np.cos(angc), np.sin(angc)], axis=1) / math.sqrt(LANE))
    w_out = w_even_out[0].astype(BF16)
    gg = row2(g_gla_out[0])
    ffn_w = [(w_ffn_up[i].astype(BF16), w_ffn_conv[i], row2(b_ffn_conv[i]), w_ffn_down[i].astype(BF16))
             for i in range(2)]

    def even_layer(h, sc1, sh1, ga1, sc2, sh2, ga2, s0f, s0b):
        af, bf, q, k, v, r, cf, cb, dmin = _inproj(h, sc1, sh1, row2(g_norm_mix[0]), wm, wz, cs, wg, bg)
        four = _position_dft(af, bf)
        of, ob, sf, sb = _gla(q, k, v, cf, cb, dmin, s0f, s0b)
        h = _mixout(h, four, of, ob, r, gg, w_out, ga1)
        h = _conv_ffn(h, sc2, sh2, ga2, row2(g_norm_ffn[0]), *ffn_w[0])
        return h, sf, sb

    zero_state = jnp.zeros((bsz, vw, kw), F32)
    h_ctx, s_f, s_b = even_layer(ctx, cx(0, 1), cx(0, 0), cx(0, 2), cx(0, 4), cx(0, 3), cx(0, 5),
                                 zero_state, zero_state)
    h_lat, _, _ = even_layer(x, lat(0, 1), lat(0, 0), lat(0, 2), lat(0, 4), lat(0, 3), lat(0, 5), s_f, s_b)

    wq = w_qkv[0].astype(BF16)
    cosf, sinf = _rope_tables(seq)
    gq, gk, gm = row2(g_q[0]), row2(g_k[0]), row2(g_norm_mix[1])
    k_c, v_c = _qkv(h_ctx, cx(1, 1), cx(1, 0), gm, wq, gq, gk, want_q=False)
    q_l, k_l, v_l = _qkv(h_lat, lat(1, 1), lat(1, 0), gm, wq, gq, gk, cosf, sinf)
    k_all = jnp.concatenate([k_c, k_l], axis=2)
    v_all = jnp.concatenate([v_c, v_l], axis=2)
    score_bound = HEAD_DIM * Q_SCALE * jnp.max(jnp.abs(g_q[0])) * jnp.max(jnp.abs(g_k[0]))
    att = lax.cond(score_bound <= ATT_PLAIN_MAX_LOG2,
                   lambda: _attention(q_l, k_all, v_all, online=False),
                   lambda: _attention(q_l, k_all, v_all, online=True))
    h_lat = _proj_res(h_lat, att, w_att_out[0].astype(BF16), lat(1, 2))
    return _conv_ffn(h_lat, lat(1, 4), lat(1, 3), lat(1, 5), row2(g_norm_ffn[1]), *ffn_w[1],
                     gfin=row2(g_norm_final))
```

```python
import functools
import math

import jax
import jax.numpy as jnp
import numpy as np
from jax import lax
from jax.experimental import pallas as pl
from jax.experimental.pallas import tpu as pltpu

F32 = jnp.float32
BF16 = jnp.bfloat16

EPS = 1e-6
N_MOD = 6
LANE = 128
SUBLANE_BF16 = 16
VMEM_LIMIT = 56 * 1024 * 1024

FOURIER_GROUPS = 4
GLA_HEADS = 4
GLA_DK = 64
GLA_DV = 128
GLA_GATE_RANK = 16
GLA_GATE_TEMP = 16.0
GLA_CHUNK = 64
HEAD_DIM = 128
ATT_KV_HEADS = 2
GRID_W = 64
ROPE_THETA = 10000.0
CONV_W = 3
Q_SCALE = HEAD_DIM ** -0.5 * math.log2(math.e)
ATT_PLAIN_MAX_LOG2 = 64.0

RESIDUAL_TILE_ROWS = 1024
PROJ_TILE_ROWS = 512
PROJ_SUB_ROWS = 256

_NT = (((1,), (1,)), ((), ()))
_TN = (((0,), (0,)), ((), ()))


def _dot(a, b):
    return jnp.dot(a, b, preferred_element_type=F32)


def _params(*sem):
    return pltpu.CompilerParams(dimension_semantics=sem, vmem_limit_bytes=VMEM_LIMIT)


def _const_spec(shape):
    nd = len(shape)
    return pl.BlockSpec(shape, lambda *_: (0,) * nd, pipeline_mode=pl.Buffered(1))


def _table(values):
    return jnp.asarray(values, F32).astype(BF16)


def _row_tile(n, cap):
    t = min(n, cap)
    assert n % t == 0, (n, t)
    return t


def _norm_mod(x, g, sc, sh):
    y = x * lax.rsqrt(jnp.mean(x * x, axis=-1, keepdims=True) + EPS)
    return y * (g * (1.0 + sc)) + sh


def _silu(x):
    return x * jax.nn.sigmoid(x)


def _mod_kernel(cv_ref, w_ref, b_ref, o_ref):
    s = _silu(cv_ref[...]).astype(BF16)
    o_ref[...] = _dot(s, w_ref[...].astype(BF16)) + b_ref[...]


def _modulation(cv, w_mod, b_mod):
    depth, d, n = w_mod.shape
    rows = cv.shape[0]
    tn = 1536
    assert n % tn == 0
    return pl.pallas_call(
        _mod_kernel,
        grid=(depth, n // tn),
        in_specs=[
            pl.BlockSpec((rows, d), lambda i, j: (0, 0)),
            pl.BlockSpec((None, d, tn), lambda i, j: (i, 0, j)),
            pl.BlockSpec((None, 1, tn), lambda i, j: (i, 0, j)),
        ],
        out_specs=pl.BlockSpec((None, rows, tn), lambda i, j: (i, 0, j)),
        out_shape=jax.ShapeDtypeStruct((depth, rows, n), F32),
        compiler_params=_params("parallel", "parallel"),
        name="adaln_mod",
    )(cv, w_mod, b_mod.reshape(depth, 1, n))


def _split3(x):
    hi = x.astype(BF16)
    r1 = x - hi.astype(F32)
    mid = r1.astype(BF16)
    lo = (r1 - mid.astype(F32)).astype(BF16)
    return hi, mid, lo


def _inproj_kernel(h_ref, sc_ref, sh_ref, g_ref, wm_ref, wz_ref, cs_ref, wg_ref, bg_ref,
                   tril_ref, triu_ref,
                   af_ref, bf_ref, q_ref, k_ref, v_ref, r_ref, cf_ref, cb_ref, dmin_ref):
    fw = FOURIER_GROUPS * LANE
    kw = GLA_HEADS * GLA_DK
    vw = GLA_HEADS * GLA_DV
    tm = h_ref.shape[0]
    sub = min(tm, PROJ_SUB_ROWS)
    tril = tril_ref[...]
    triu = triu_ref[...]

    def project(s):
        rows = slice(s * sub, (s + 1) * sub)
        a = _norm_mod(h_ref[rows, :], g_ref[...], sc_ref[...], sh_ref[...]).astype(BF16)
        return _dot(a, wm_ref[...]), _dot(a, wz_ref[...])

    def finish(s, p, z, dmin):
        rows = slice(s * sub, (s + 1) * sub)
        for g in range(FOURIER_GROUPS):
            ab = _dot(p[:, g * LANE:(g + 1) * LANE].astype(BF16), cs_ref[...])
            af_ref[rows, g * LANE:(g + 1) * LANE] = ab[:, :LANE].astype(BF16)
            bf_ref[rows, g * LANE:(g + 1) * LANE] = ab[:, LANE:].astype(BF16)
        q_ref[rows, :] = p[:, fw:fw + kw] * (GLA_DK ** -0.5)
        k_ref[rows, :] = p[:, fw + kw:fw + 2 * kw]
        v_ref[rows, :] = p[:, fw + 2 * kw:fw + 2 * kw + vw].astype(BF16)
        r_ref[rows, :] = p[:, fw + 2 * kw + vw:fw + 2 * kw + 2 * vw].astype(r_ref.dtype)
        zz = _dot(z.astype(BF16), wg_ref[...]) + bg_ref[...]
        loga = (jnp.minimum(zz, 0.0) - jnp.log1p(jnp.exp(-jnp.abs(zz)))) * (1.0 / GLA_GATE_TEMP)
        accf = None
        accb = None
        for term in _split3(loga):
            tf = _dot(tril, term[:, :kw])
            tb = _dot(triu, term[:, kw:])
            accf = tf if accf is None else accf + tf
            accb = tb if accb is None else accb + tb
        cf_ref[rows, :] = accf
        cb_ref[rows, :] = accb
        for c in range(sub // GLA_CHUNK):
            tot = jnp.minimum(accf[(c + 1) * GLA_CHUNK - 1:(c + 1) * GLA_CHUNK],
                              accb[c * GLA_CHUNK:c * GLA_CHUNK + 1])
            dmin = tot if dmin is None else jnp.minimum(dmin, tot)
        return dmin

    dmin = None
    pz = project(0)
    for s in range(tm // sub):
        pz_next = project(s + 1) if (s + 1) * sub < tm else None
        dmin = finish(s, *pz, dmin)
        pz = pz_next
    dmin_ref[...] = jnp.broadcast_to(dmin, dmin_ref.shape)


def _inproj(h, sc, sh, g, wm, wz, cs, wg, bg):
    b, l, d = h.shape
    tm = _row_tile(l, PROJ_TILE_ROWS)
    ic = np.arange(min(tm, PROJ_SUB_ROWS))
    same_chunk = (ic[:, None] // GLA_CHUNK) == (ic[None, :] // GLA_CHUNK)
    tril = jnp.asarray(same_chunk & (ic[:, None] >= ic[None, :]), BF16)
    triu = jnp.asarray(same_chunk & (ic[:, None] <= ic[None, :]), BF16)
    per_batch = sc.shape[0] == b and b > 1
    mod_idx = (lambda bi, i: (bi, 0, 0)) if per_batch else (lambda bi, i: (0, 0, 0))
    row = lambda w: pl.BlockSpec((None, tm, w), lambda bi, i: (bi, i, 0))
    outs = [(512, BF16), (512, BF16), (256, F32), (256, F32), (512, BF16), (512, BF16), (256, F32), (256, F32)]
    kw = GLA_HEADS * GLA_DK
    return pl.pallas_call(
        _inproj_kernel,
        grid=(b, l // tm),
        in_specs=[
            row(d),
            pl.BlockSpec((None, 1, d), mod_idx),
            pl.BlockSpec((None, 1, d), mod_idx),
            _const_spec(g.shape), _const_spec(wm.shape), _const_spec(wz.shape), _const_spec(cs.shape),
            _const_spec(wg.shape), _const_spec(bg.shape), _const_spec(tril.shape), _const_spec(triu.shape),
        ],
        out_specs=[row(w) for w, _ in outs] + [pl.BlockSpec((None, None, 8, kw), lambda bi, i: (bi, i, 0, 0))],
        out_shape=[jax.ShapeDtypeStruct((b, l, w), dt) for w, dt in outs]
        + [jax.ShapeDtypeStruct((b, l // tm, 8, kw), F32)],
        compiler_params=_params("parallel", "parallel"),
        name="even_inproj",
    )(h, sc, sh, g, wm, wz, cs, wg, bg, tril, triu)


def _dft_dense_kernel(a_ref, b_ref, c_ref, s_ref, o_ref):
    o_ref[...] = (_dot(c_ref[...], a_ref[...]) - _dot(s_ref[...], b_ref[...])).astype(o_ref.dtype)


def _dft_dense(af, bf):
    b, l, w = af.shape
    n = np.arange(l)
    ang = 2.0 * np.pi * ((n[:, None] * n[None, :]) % l) / l
    c = _table(np.cos(ang) / math.sqrt(l))
    s = _table(np.sin(ang) / math.sqrt(l))
    blk = pl.BlockSpec((None, l, w), lambda bi: (bi, 0, 0))
    return pl.pallas_call(
        _dft_dense_kernel,
        grid=(b,),
        in_specs=[blk, blk, _const_spec(c.shape), _const_spec(s.shape)],
        out_specs=blk,
        out_shape=jax.ShapeDtypeStruct((b, l, w), BF16),
        compiler_params=_params("parallel"),
        name="dft_dense",
    )(af, bf, c, s)


DFT_N2 = LANE
DFT_NB = 16
DFT_KB = 16


def _dft_stage1_kernel(a_ref, b_ref, m_ref, tc_ref, ts_ref, y_ref):
    n1 = a_ref.shape[0]
    y = _dot(m_ref[...], jnp.concatenate([a_ref[...], b_ref[...]], axis=0))
    for j in range(DFT_NB):
        tc = tc_ref[:, j * LANE:(j + 1) * LANE]
        ts = ts_ref[:, j * LANE:(j + 1) * LANE]
        for g in range(FOURIER_GROUPS):
            sl = slice((j * FOURIER_GROUPS + g) * LANE, (j * FOURIER_GROUPS + g + 1) * LANE)
            yr = y[:n1, sl]
            yi = y[n1:, sl]
            y_ref[:n1, sl] = (yr * tc + yi * ts).astype(BF16)
            y_ref[n1:, sl] = (yi * tc - yr * ts).astype(BF16)


def _dft_stage2_kernel(yr_ref, yi_ref, c_ref, s_ref, o_ref):
    w = yr_ref.shape[-1]
    for kk in range(DFT_KB):
        o = _dot(c_ref[...], yr_ref[kk]) + _dot(s_ref[...], yi_ref[kk])
        o_ref[:, kk * w:(kk + 1) * w] = o.astype(o_ref.dtype)


def _dft_factored(af, bf):
    b, l, w = af.shape
    n1 = l // DFT_N2
    assert l % DFT_N2 == 0 and n1 % SUBLANE_BF16 == 0 and n1 % DFT_KB == 0
    k = np.arange(n1)
    ang1 = 2.0 * np.pi * ((k[:, None] * k[None, :]) % n1) / n1
    c1, s1 = np.cos(ang1), np.sin(ang1)
    m1 = _table(np.block([[c1, -s1], [-s1, -c1]]))
    n2 = np.arange(DFT_N2)
    angt = 2.0 * np.pi * (k[:, None] * n2[None, :]) / l
    scale = 1.0 / math.sqrt(l)
    tc = jnp.asarray(np.repeat(np.cos(angt) * scale, LANE, axis=1), F32)
    ts = jnp.asarray(np.repeat(np.sin(angt) * scale, LANE, axis=1), F32)
    ang2 = 2.0 * np.pi * ((n2[:, None] * n2[None, :]) % DFT_N2) / DFT_N2
    c2 = _table(np.cos(ang2))
    s2 = _table(np.sin(ang2))

    a3 = af.reshape(b, n1, DFT_N2 * w)
    b3 = bf.reshape(b, n1, DFT_N2 * w)
    in_blk = pl.BlockSpec((None, n1, DFT_NB * w), lambda bi, j: (bi, 0, j))
    y = pl.pallas_call(
        _dft_stage1_kernel,
        grid=(b, DFT_N2 // DFT_NB),
        in_specs=[
            in_blk, in_blk, _const_spec(m1.shape),
            pl.BlockSpec((n1, DFT_NB * LANE), lambda bi, j: (0, j)),
            pl.BlockSpec((n1, DFT_NB * LANE), lambda bi, j: (0, j)),
        ],
        out_specs=pl.BlockSpec((None, 2 * n1, DFT_NB * w), lambda bi, j: (bi, 0, j)),
        out_shape=jax.ShapeDtypeStruct((b, 2 * n1, DFT_N2 * w), BF16),
        compiler_params=_params("parallel", "parallel"),
        name="dft_stage1",
    )(a3, b3, m1, tc, ts)

    y4 = y.reshape(b, 2 * n1, DFT_N2, w)
    nkb = n1 // DFT_KB
    out = pl.pallas_call(
        _dft_stage2_kernel,
        grid=(b, nkb),
        in_specs=[
            pl.BlockSpec((None, DFT_KB, DFT_N2, w), lambda bi, i: (bi, i, 0, 0)),
            pl.BlockSpec((None, DFT_KB, DFT_N2, w), lambda bi, i: (bi, nkb + i, 0, 0)),
            _const_spec(c2.shape), _const_spec(s2.shape),
        ],
        out_specs=pl.BlockSpec((None, DFT_N2, DFT_KB * w), lambda bi, i: (bi, 0, i)),
        out_shape=jax.ShapeDtypeStruct((b, DFT_N2, n1 * w), BF16),
        compiler_params=_params("parallel", "parallel"),
        name="dft_stage2",
    )(y4, y4, c2, s2)
    return out.reshape(b, l, w)


def _position_dft(af, bf):
    l = af.shape[1]
    if l % (DFT_N2 * SUBLANE_BF16) == 0:
        return _dft_factored(af, bf)
    return _dft_dense(af, bf)


GLA_FACTOR_MAX_DECAY = 80.0


def _gla_chunk(q_ref, k_ref, v_ref, c_ref, r0, s_ref, head_masks, tri_mask, state_mask, pair_sum, forward,
               factored):
    c = GLA_CHUNK
    rows = pl.ds(r0, c)
    q, k, v, cum = q_ref[rows, :], k_ref[rows, :], v_ref[rows, :], c_ref[rows, :]
    tot = cum[c - 1:c] if forward else cum[0:1]
    qe = q * jnp.exp(cum)
    kd = (k * jnp.exp(tot - cum)).astype(BF16)
    if factored:
        ke = (k * jnp.exp(-cum)).astype(BF16)
        q_stack = jnp.concatenate([qe * hm for hm in head_masks], axis=0).astype(BF16)
        att = lax.dot_general(q_stack, ke, _NT, preferred_element_type=F32)
        att = jnp.where(tri_mask, att, 0.0).astype(BF16)
        o_full = _dot(att, v)
        o_intra = jnp.concatenate(
            [o_full[h * c:(h + 1) * c, h * GLA_DV:(h + 1) * GLA_DV] for h in range(GLA_HEADS)], axis=1)
    else:
        ri = lax.broadcasted_iota(jnp.int32, (c, 1), 0)
        v32 = v.astype(F32)

        def row_of(x, j):
            return jnp.sum(jnp.where(ri == j, x, 0.0), axis=0, keepdims=True)

        def key_row(j, o):
            kj, cj, vj = row_of(k, j), row_of(cum, j), row_of(v32, j)
            live = (ri >= j) if forward else (ri <= j)
            e = q * kj * jnp.exp(jnp.where(live, cum - cj, -jnp.inf))
            return o + _dot(e.astype(BF16), pair_sum) * vj

        o_intra = lax.fori_loop(0, c, key_row, jnp.zeros((c, v.shape[1]), F32))
    s = s_ref[...]
    o_inter = lax.dot_general(qe.astype(BF16), s.astype(BF16), _NT, preferred_element_type=F32)
    upd = lax.dot_general(v, kd, _TN, preferred_element_type=F32)
    s_ref[...] = s * jnp.exp(tot) + jnp.where(state_mask, upd, 0.0)
    return o_intra + o_inter


def _gla_kernel(qf_ref, kf_ref, vf_ref, cf_ref, qb_ref, kb_ref, vb_ref, cb_ref, s0f_ref, s0b_ref,
                of_ref, ob_ref, sf_ref, sb_ref, *, factored):
    t = pl.program_id(1)
    c = GLA_CHUNK
    kw = GLA_HEADS * GLA_DK
    vw = GLA_HEADS * GLA_DV

    @pl.when(t == 0)
    def _():
        sf_ref[...] = s0f_ref[...]
        sb_ref[...] = s0b_ref[...]

    lane = lax.broadcasted_iota(jnp.int32, (1, kw), 1)
    head_masks = [(lane // GLA_DK == h).astype(F32) for h in range(GLA_HEADS)]
    ri = lax.broadcasted_iota(jnp.int32, (GLA_HEADS * c, c), 0) % c
    ci = lax.broadcasted_iota(jnp.int32, (GLA_HEADS * c, c), 1)
    tril = ci <= ri
    triu = ci >= ri
    sr = lax.broadcasted_iota(jnp.int32, (vw, kw), 0) // GLA_DV
    scol = lax.broadcasted_iota(jnp.int32, (vw, kw), 1) // GLA_DK
    state_mask = sr == scol
    pr = lax.broadcasted_iota(jnp.int32, (kw, vw), 0) // GLA_DK
    pc = lax.broadcasted_iota(jnp.int32, (kw, vw), 1) // GLA_DV
    pair_sum = (pr == pc).astype(BF16)

    nc = qf_ref.shape[0] // c
    for i in range(nc):
        of_ref[pl.ds(i * c, c), :] = _gla_chunk(
            qf_ref, kf_ref, vf_ref, cf_ref, i * c, sf_ref, head_masks, tril, state_mask, pair_sum, True, factored
        ).astype(of_ref.dtype)
        rb = (nc - 1 - i) * c
        ob_ref[pl.ds(rb, c), :] = _gla_chunk(
            qb_ref, kb_ref, vb_ref, cb_ref, rb, sb_ref, head_masks, triu, state_mask, pair_sum, False, factored
        ).astype(ob_ref.dtype)


def _gla_call(q, k, v, cf, cb, s0f, s0b, factored):
    b, l, kw = q.shape
    vw = v.shape[-1]
    tl = _row_tile(l, 512)
    nt = l // tl
    fwd = lambda w: pl.BlockSpec((None, tl, w), lambda bi, t: (bi, t, 0))
    bwd = lambda w: pl.BlockSpec((None, tl, w), lambda bi, t: (bi, nt - 1 - t, 0))
    st = pl.BlockSpec((None, vw, kw), lambda bi, t: (bi, 0, 0))
    return pl.pallas_call(
        functools.partial(_gla_kernel, factored=factored),
        grid=(b, nt),
        in_specs=[fwd(kw), fwd(kw), fwd(vw), fwd(kw), bwd(kw), bwd(kw), bwd(vw), bwd(kw), st, st],
        out_specs=[fwd(vw), bwd(vw), st, st],
        out_shape=[jax.ShapeDtypeStruct((b, l, vw), BF16), jax.ShapeDtypeStruct((b, l, vw), BF16),
                   jax.ShapeDtypeStruct((b, vw, kw), F32), jax.ShapeDtypeStruct((b, vw, kw), F32)],
        compiler_params=_params("parallel", "arbitrary"),
        name="gla_scan" if factored else "gla_scan_pairwise",
    )(q, k, v, cf, q, k, v, cb, s0f, s0b)


def _gla(q, k, v, cf, cb, dmin, s0f, s0b):
    args = (q, k, v, cf, cb, s0f, s0b)
    return lax.cond(jnp.min(dmin) >= -GLA_FACTOR_MAX_DECAY,
                    lambda: _gla_call(*args, factored=True),
                    lambda: _gla_call(*args, factored=False))


def _mixout_kernel(h_ref, f_ref, of_ref, ob_ref, r_ref, gg_ref, w_ref, ga_ref, o_ref):
    fw = f_ref.shape[-1]
    o = of_ref[...].astype(F32) + ob_ref[...].astype(F32)
    parts = []
    for hd in range(GLA_HEADS):
        oh = o[:, hd * GLA_DV:(hd + 1) * GLA_DV]
        parts.append(oh * lax.rsqrt(jnp.mean(oh * oh, axis=-1, keepdims=True) + EPS))
    on = jnp.concatenate(parts, axis=1) * gg_ref[...]
    on = (on * _silu(r_ref[...].astype(F32))).astype(BF16)
    y = _dot(f_ref[...], w_ref[:fw, :]) + _dot(on, w_ref[fw:, :])
    o_ref[...] = h_ref[...] + ga_ref[...] * y


def _mixout(h, four, of, ob, r, gg, w, ga):
    b, l, d = h.shape
    tm = _row_tile(l, RESIDUAL_TILE_ROWS)
    per_batch = ga.shape[0] == b and b > 1
    mod_idx = (lambda bi, i: (bi, 0, 0)) if per_batch else (lambda bi, i: (0, 0, 0))
    row = lambda w_: pl.BlockSpec((None, tm, w_), lambda bi, i: (bi, i, 0))
    return pl.pallas_call(
        _mixout_kernel,
        grid=(b, l // tm),
        in_specs=[row(d), row(four.shape[-1]), row(of.shape[-1]), row(ob.shape[-1]), row(r.shape[-1]),
                  _const_spec(gg.shape), _const_spec(w.shape), pl.BlockSpec((None, 1, d), mod_idx)],
        out_specs=row(d),
        out_shape=jax.ShapeDtypeStruct((b, l, d), F32),
        compiler_params=_params("parallel", "parallel"),
        name="even_mixout",
    )(h, four, of, ob, r, gg, w, ga)


FFN_HALO = SUBLANE_BF16
FFN_TILE_ROWS = 512
FFN_TF = 256
FFN_DOWN_CHUNKS = 3


def _ffn_kernel(h_ref, hp_ref, hn_ref, sc_ref, sh_ref, ga_ref, g_ref, wup_ref, wc_ref, bc_ref, wdn_ref,
                *rest, final):
    if final:
        gfin_ref, o_ref, fext_ref, act_ref = rest
    else:
        o_ref, fext_ref, act_ref = rest
    i = pl.program_id(1)
    last = pl.num_programs(1) - 1
    tm = h_ref.shape[0]
    dff = wdn_ref.shape[0]
    hal = FFN_HALO
    g, sc, sh = g_ref[...], sc_ref[...], sh_ref[...]
    h = h_ref[...]
    fp = _norm_mod(hp_ref[...], g, sc, sh) * (i > 0).astype(F32)
    fn = _norm_mod(hn_ref[...], g, sc, sh) * (i < last).astype(F32)
    fext_ref[0:hal, :] = fp.astype(BF16)
    fext_ref[hal:hal + tm, :] = _norm_mod(h, g, sc, sh).astype(BF16)
    fext_ref[hal + tm:, :] = fn.astype(BF16)
    acc = None
    nch = dff // FFN_TF
    flushed = 0

    def cols_of(c, half):
        return slice(half * dff + c * FFN_TF, half * dff + (c + 1) * FFN_TF)

    pvals = {}

    def up(c):
        for half in range(2):
            pvals[(c, half)] = _dot(fext_ref[...], wup_ref[:, cols_of(c, half)])

    def conv(c, half):
        p = pvals.pop((c, half))
        n = p.shape[0]
        cs = cols_of(c, half)
        return (pltpu.roll(p, 1, 0)[hal:hal + tm] * wc_ref[0:1, cs]
                + p[hal:hal + tm] * wc_ref[1:2, cs]
                + pltpu.roll(p, n - 1, 0)[hal:hal + tm] * wc_ref[2:3, cs]
                + bc_ref[:, cs])

    def down(done):
        nonlocal acc, flushed
        ks = slice(flushed * FFN_TF, done * FFN_TF)
        d = _dot(act_ref[:, ks], wdn_ref[ks, :])
        acc = d if acc is None else acc + d
        flushed = done

    up(0)
    for c in range(nch):
        if c + 1 < nch:
            up(c + 1)
        if c - flushed >= FFN_DOWN_CHUNKS or (c == nch - 1 and c > flushed):
            down(c)
        act_ref[:, c * FFN_TF:(c + 1) * FFN_TF] = (_silu(conv(c, 0)) * conv(c, 1)).astype(BF16)
    down(nch)
    out = h + ga_ref[...] * acc
    if final:
        out = out * lax.rsqrt(jnp.mean(out * out, axis=-1, keepdims=True) + EPS) * gfin_ref[...]
    o_ref[...] = out


def _conv_ffn(h, sc, sh, ga, g, wup, wc, bc, wdn, gfin=None):
    b, l, d = h.shape
    tm = _row_tile(l, FFN_TILE_ROWS)
    assert tm % FFN_HALO == 0 and wdn.shape[0] % FFN_TF == 0 and CONV_W == 3
    per_batch = ga.shape[0] == b and b > 1
    mod_idx = (lambda bi, i: (bi, 0, 0)) if per_batch else (lambda bi, i: (0, 0, 0))
    hb = tm // FFN_HALO
    nh = l // FFN_HALO
    mod = pl.BlockSpec((None, 1, d), mod_idx)
    in_specs = [
        pl.BlockSpec((None, tm, d), lambda bi, i: (bi, i, 0)),
        pl.BlockSpec((None, FFN_HALO, d), lambda bi, i: (bi, jnp.maximum(i * hb - 1, 0), 0)),
        pl.BlockSpec((None, FFN_HALO, d), lambda bi, i: (bi, jnp.minimum((i + 1) * hb, nh - 1), 0)),
        mod, mod, mod,
        _const_spec(g.shape), _const_spec(wup.shape), _const_spec(wc.shape), _const_spec(bc.shape),
        _const_spec(wdn.shape),
    ]
    args = [h, h, h, sc, sh, ga, g, wup, wc, bc, wdn]
    if gfin is not None:
        in_specs.append(_const_spec(gfin.shape))
        args.append(gfin)
    return pl.pallas_call(
        functools.partial(_ffn_kernel, final=gfin is not None),
        grid=(b, l // tm),
        in_specs=in_specs,
        out_specs=pl.BlockSpec((None, tm, d), lambda bi, i: (bi, i, 0)),
        out_shape=jax.ShapeDtypeStruct((b, l, d), F32),
        scratch_shapes=[pltpu.VMEM((tm + 2 * FFN_HALO, d), BF16), pltpu.VMEM((tm, wdn.shape[0]), BF16)],
        compiler_params=_params("parallel", "parallel"),
        name="conv_ffn_final" if gfin is not None else "conv_ffn",
    )(*args)


def _head_norm(t, g):
    return t * lax.rsqrt(jnp.mean(t * t, axis=-1, keepdims=True) + EPS) * g


def _rope(t, cosf, sinf):
    return t * cosf + pltpu.roll(t, HEAD_DIM // 2, 1) * sinf


def _qkv_kernel(h_ref, sc_ref, sh_ref, g_ref, w_ref, gq_ref, gk_ref, *rest, n_q, rope):
    if rope:
        cos_ref, sin_ref = rest[:2]
        rest = rest[2:]
    if n_q:
        q_ref, k_ref, v_ref = rest
    else:
        k_ref, v_ref = rest
    col = lambda j: slice(j * HEAD_DIM, (j + 1) * HEAD_DIM)
    w_q = w_ref.shape[1] - 2 * ATT_KV_HEADS * HEAD_DIM
    c0 = 0 if n_q else w_q
    tm = h_ref.shape[0]
    sub = min(tm, PROJ_SUB_ROWS)

    def project(s):
        rows = slice(s * sub, (s + 1) * sub)
        a = _norm_mod(h_ref[rows, :], g_ref[...], sc_ref[...], sh_ref[...]).astype(BF16)
        return _dot(a, w_ref[:, c0:])

    def finish(s, p):
        rows = slice(s * sub, (s + 1) * sub)
        cosf = cos_ref[rows, :] if rope else None
        sinf = sin_ref[rows, :] if rope else None
        for hd in range(n_q):
            t = _head_norm(p[:, col(hd)], gq_ref[...])
            if rope:
                t = _rope(t, cosf, sinf)
            q_ref[hd, rows, :] = (t * Q_SCALE).astype(BF16)
        for hd in range(ATT_KV_HEADS):
            t = _head_norm(p[:, col(n_q + hd)], gk_ref[...])
            if rope:
                t = _rope(t, cosf, sinf)
            k_ref[hd, rows, :] = t.astype(BF16)
            v_ref[hd, rows, :] = p[:, col(n_q + ATT_KV_HEADS + hd)].astype(BF16)

    p = project(0)
    for s in range(tm // sub):
        p_next = project(s + 1) if (s + 1) * sub < tm else None
        finish(s, p)
        p = p_next


def _qkv(h, sc, sh, g, w, gq, gk, cosf=None, sinf=None, want_q=True):
    b, l, d = h.shape
    tm = _row_tile(l, PROJ_TILE_ROWS)
    n_heads = (w.shape[1] - 2 * ATT_KV_HEADS * HEAD_DIM) // HEAD_DIM
    n_q = n_heads if want_q else 0
    rope = cosf is not None
    per_batch = sc.shape[0] == b and b > 1
    mod_idx = (lambda bi, i: (bi, 0, 0)) if per_batch else (lambda bi, i: (0, 0, 0))
    in_specs = [
        pl.BlockSpec((None, tm, d), lambda bi, i: (bi, i, 0)),
        pl.BlockSpec((None, 1, d), mod_idx), pl.BlockSpec((None, 1, d), mod_idx),
        _const_spec(g.shape), _const_spec(w.shape), _const_spec(gq.shape), _const_spec(gk.shape),
    ]
    args = [h, sc, sh, g, w, gq, gk]
    if rope:
        in_specs += [pl.BlockSpec((tm, HEAD_DIM), lambda bi, i: (i, 0))] * 2
        args += [cosf, sinf]
    heads = lambda n: pl.BlockSpec((None, n, tm, HEAD_DIM), lambda bi, i: (bi, 0, i, 0))
    shape = lambda n: jax.ShapeDtypeStruct((b, n, l, HEAD_DIM), BF16)
    out_specs = [heads(ATT_KV_HEADS), heads(ATT_KV_HEADS)]
    out_shape = [shape(ATT_KV_HEADS), shape(ATT_KV_HEADS)]
    if n_q:
        out_specs = [heads(n_q)] + out_specs
        out_shape = [shape(n_q)] + out_shape
    return pl.pallas_call(
        functools.partial(_qkv_kernel, n_q=n_q, rope=rope),
        grid=(b, l // tm),
        in_specs=in_specs,
        out_specs=out_specs,
        out_shape=out_shape,
        compiler_params=_params("parallel", "parallel"),
        name="qkv_rope" if rope else "kv_ctx",
    )(*args)


def _attn_kernel(q_ref, k_ref, v_ref, o_ref, *scratch, tk, online):
    grp, tq, hd = q_ref.shape
    nlt = tk // LANE
    nk = k_ref.shape[0] // tk
    assert grp % 2 == 0 and hd == LANE
    l_ref, acc_ref = scratch[:grp], scratch[grp:2 * grp]
    p_ref = scratch[2 * grp:2 * grp + 2]
    if online:
        rest = scratch[2 * grp + 2:]
        m_ref, al_ref, s_ref, mx_ref = rest[:grp], rest[grp:2 * grp], rest[2 * grp:2 * grp + 2], rest[2 * grp + 2:]
    for g in range(grp):
        l_ref[g][...] = jnp.zeros(l_ref[g].shape, F32)
        acc_ref[g][...] = jnp.zeros(acc_ref[g].shape, F32)
        if online:
            m_ref[g][...] = jnp.full(m_ref[g].shape, -jnp.inf, F32)

    def chunk(ref, j):
        if isinstance(j, int):
            return ref[j * tk:(j + 1) * tk, :]
        start = pl.multiple_of(jnp.clip(j, 0, nk - 1) * tk, tk)
        return ref[pl.ds(start, tk), :]

    def lane_tiles(x):
        return [x[:, t * LANE:(t + 1) * LANE] for t in range(nlt)]

    def scores(g, j, slot):
        s = lax.dot_general(q_ref[g], chunk(k_ref, j), _NT, preferred_element_type=F32)
        if online:
            s_ref[slot][...] = s
            mx_ref[slot][...] = functools.reduce(jnp.maximum, lane_tiles(s))
        else:
            p = jnp.exp2(s)
            p_ref[slot][...] = p.astype(BF16)
            l_ref[g][...] += functools.reduce(jnp.add, lane_tiles(p))

    def softmax(g, slot):
        m_old = m_ref[g][...]
        m_new = jnp.maximum(m_old, jnp.max(mx_ref[slot][...], axis=-1, keepdims=True))
        alpha = jnp.exp2(m_old - m_new)
        al_ref[g][...] = alpha
        m_ref[g][...] = m_new
        ps = None
        for t in range(nlt):
            p = jnp.exp2(s_ref[slot][:, t * LANE:(t + 1) * LANE] - m_new)
            p_ref[slot][:, t * LANE:(t + 1) * LANE] = p.astype(BF16)
            ps = p if ps is None else ps + p
        l_ref[g][...] = alpha * l_ref[g][...] + ps

    def values(g, j, slot):
        pv = _dot(p_ref[slot][...], chunk(v_ref, j))
        if online:
            acc_ref[g][...] = al_ref[g][...] * acc_ref[g][...] + pv
        else:
            acc_ref[g][...] += pv

    scores(0, 0, 0)
    if online:
        al_ref[grp - 1][...] = jnp.ones(al_ref[grp - 1].shape, F32)
        p_ref[(grp - 1) % 2][...] = jnp.zeros(p_ref[(grp - 1) % 2].shape, BF16)

        def body(j, carry):
            for g in range(grp):
                scores((g + 1) % grp, j + (g + 1) // grp, (g + 1) % 2)
                softmax(g, g % 2)
                values((g - 1) % grp, j + (g - 1) // grp, (g - 1) % 2)
            return carry

        lax.fori_loop(0, nk, body, 0)
        values(grp - 1, nk - 1, (grp - 1) % 2)
    else:
        for j in range(nk):
            for g in range(grp):
                if g + 1 < grp:
                    scores(g + 1, j, (g + 1) % 2)
                elif j + 1 < nk:
                    scores(0, j + 1, 0)
                values(g, j, g % 2)
    for g in range(grp):
        out = acc_ref[g][...] / jnp.sum(l_ref[g][...], axis=-1, keepdims=True)
        o_ref[:, g * hd:(g + 1) * hd] = out.astype(o_ref.dtype)


def _attention(q, k, v, online):
    b, h, s, hd = q.shape
    hkv, lk = k.shape[1], k.shape[2]
    grp = h // hkv
    tq = _row_tile(s, 512)
    tk = next(t for t in (768, 512, 256, 128) if lk % t == 0)
    kv = pl.BlockSpec((None, None, lk, hd), lambda bi, kh, i: (bi, kh, 0, 0))
    scratch = [pltpu.VMEM((tq, LANE), F32)] * (2 * grp) + [pltpu.VMEM((tq, tk), BF16)] * 2
    if online:
        scratch += ([pltpu.VMEM((tq, LANE), F32)] * (2 * grp) + [pltpu.VMEM((tq, tk), F32)] * 2
                    + [pltpu.VMEM((tq, LANE), F32)] * 2)
    return pl.pallas_call(
        functools.partial(_attn_kernel, tk=tk, online=online),
        grid=(b, hkv, s // tq),
        in_specs=[pl.BlockSpec((None, grp, tq, hd), lambda bi, kh, i: (bi, kh, i, 0)), kv, kv],
        out_specs=pl.BlockSpec((None, tq, grp * hd), lambda bi, kh, i: (bi, i, kh)),
        out_shape=jax.ShapeDtypeStruct((b, s, h * hd), BF16),
        scratch_shapes=scratch,
        compiler_params=_params("parallel", "parallel", "arbitrary"),
        name="gqa_flash_online" if online else "gqa_flash",
    )(q, k, v)


def _proj_res_kernel(h_ref, x_ref, w_ref, ga_ref, o_ref):
    o_ref[...] = h_ref[...] + ga_ref[...] * _dot(x_ref[...], w_ref[...])


def _proj_res(h, x, w, ga):
    b, l, d = h.shape
    tm = _row_tile(l, RESIDUAL_TILE_ROWS)
    row = lambda w_: pl.BlockSpec((None, tm, w_), lambda bi, i: (bi, i, 0))
    return pl.pallas_call(
        _proj_res_kernel,
        grid=(b, l // tm),
        in_specs=[row(d), row(x.shape[-1]), _const_spec(w.shape),
                  pl.BlockSpec((None, 1, d), lambda bi, i: (bi, 0, 0))],
        out_specs=row(d),
        out_shape=jax.ShapeDtypeStruct((b, l, d), F32),
        compiler_params=_params("parallel", "parallel"),
        name="att_out",
    )(h, x, w, ga)


def _rope_tables(n_tokens):
    rows = n_tokens // GRID_W
    half = HEAD_DIM // 2
    r, c = jnp.meshgrid(jnp.arange(rows), jnp.arange(GRID_W), indexing='ij')
    inv = ROPE_THETA ** (-jnp.arange(0, half, 2, dtype=F32) / half)
    ang = jnp.concatenate([r.reshape(-1, 1).astype(F32) * inv,
                           c.reshape(-1, 1).astype(F32) * inv], axis=-1)
    cos, sin = jnp.cos(ang), jnp.sin(ang)
    return jnp.concatenate([cos, cos], axis=-1), jnp.concatenate([-sin, sin], axis=-1)


def kernel(x, c, ctx, c_ctx, w_mod, b_mod, g_norm_mix, g_norm_ffn, g_norm_final, w_even_in, w_gla_gate,
           b_gla_gate, g_gla_out, w_even_out, w_qkv, g_q, g_k, w_att_out, w_ffn_up, w_ffn_conv, b_ffn_conv,
           w_ffn_down):
    bsz, seq, d = x.shape
    assert w_mod.shape[0] == 2, "two layers: one even (Fourier || GLA) and one odd (attention)"
    fw = FOURIER_GROUPS * LANE
    kw = GLA_HEADS * GLA_DK
    vw = GLA_HEADS * GLA_DV
    main_w = fw + 2 * kw + 2 * vw

    rows = -(-(bsz + 1) // 8) * 8
    cv = jnp.zeros((rows, d), F32).at[:bsz].set(c).at[bsz].set(c_ctx)
    mod = _modulation(cv, w_mod, b_mod).reshape(2, rows, N_MOD, d)
    lat = lambda i, j: mod[i, :bsz, j][:, None, :]
    cx = lambda i, j: mod[i, bsz:bsz + 1, j][:, None, :]
    row2 = lambda v: v.reshape(1, -1)

    w_in = w_even_in[0]
    wm = w_in[:, :main_w].astype(BF16)
    wz = jnp.zeros((d, LANE), F32).at[:, :GLA_GATE_RANK].set(w_in[:, main_w:]).astype(BF16)
    wg = jnp.zeros((LANE, 2 * kw), F32).at[:GLA_GATE_RANK].set(
        jnp.concatenate([w_gla_gate[0, 0], w_gla_gate[0, 1]], axis=-1)).astype(BF16)
    bg = b_gla_gate[0].reshape(1, 2 * kw)
    nch = np.arange(LANE)
    angc = 2.0 * np.pi * ((nch[:, None] * nch[None, :]) % LANE) / LANE
    cs = _table(np.concatenate([np.cos(angc), np.sin(angc)], axis=1) / math.sqrt(LANE))
    w_out = w_even_out[0].astype(BF16)
    gg = row2(g_gla_out[0])
    ffn_w = [(w_ffn_up[i].astype(BF16), w_ffn_conv[i], row2(b_ffn_conv[i]), w_ffn_down[i].astype(BF16))
             for i in range(2)]

    def even_layer(h, sc1, sh1, ga1, sc2, sh2, ga2, s0f, s0b):
        af, bf, q, k, v, r, cf, cb, dmin = _inproj(h, sc1, sh1, row2(g_norm_mix[0]), wm, wz, cs, wg, bg)
        four = _position_dft(af, bf)
        of, ob, sf, sb = _gla(q, k, v, cf, cb, dmin, s0f, s0b)
        h = _mixout(h, four, of, ob, r, gg, w_out, ga1)
        h = _conv_ffn(h, sc2, sh2, ga2, row2(g_norm_ffn[0]), *ffn_w[0])
        return h, sf, sb

    zero_state = jnp.zeros((bsz, vw, kw), F32)
    h_ctx, s_f, s_b = even_layer(ctx, cx(0, 1), cx(0, 0), cx(0, 2), cx(0, 4), cx(0, 3), cx(0, 5),
                                 zero_state, zero_state)
    h_lat, _, _ = even_layer(x, lat(0, 1), lat(0, 0), lat(0, 2), lat(0, 4), lat(0, 3), lat(0, 5), s_f, s_b)

    wq = w_qkv[0].astype(BF16)
    cosf, sinf = _rope_tables(seq)
    gq, gk, gm = row2(g_q[0]), row2(g_k[0]), row2(g_norm_mix[1])
    k_c, v_c = _qkv(h_ctx, cx(1, 1), cx(1, 0), gm, wq, gq, gk, want_q=False)
    q_l, k_l, v_l = _qkv(h_lat, lat(1, 1), lat(1, 0), gm, wq, gq, gk, cosf, sinf)
    k_all = jnp.concatenate([k_c, k_l], axis=2)
    v_all = jnp.concatenate([v_c, v_l], axis=2)
    score_bound = HEAD_DIM * Q_SCALE * jnp.max(jnp.abs(g_q[0])) * jnp.max(jnp.abs(g_k[0]))
    att = lax.cond(score_bound <= ATT_PLAIN_MAX_LOG2,
                   lambda: _attention(q_l, k_all, v_all, online=False),
                   lambda: _attention(q_l, k_all, v_all, online=True))
    h_lat = _proj_res(h_lat, att, w_att_out[0].astype(BF16), lat(1, 2))
    return _conv_ffn(h_lat, lat(1, 4), lat(1, 3), lat(1, 5), row2(g_norm_ffn[1]), *ffn_w[1],
                     gfin=row2(g_norm_final))
```
